```python
import jax, jax.numpy as jnp
from jax import lax
import numpy as np

D_MODEL = 1024
BATCH = 8
SEQ = 8192
DEPTH = 2

D_MIX = D_MODEL
D_POOL = D_MIX // 2
POOL_WINDOWS = (2, 4, 8, 16)
N_POOL_GROUPS = len(POOL_WINDOWS)
POOL_GROUP = D_POOL // N_POOL_GROUPS
D_ATTN = D_MIX - D_POOL
N_HEADS = 8
HEAD_DIM = D_ATTN // N_HEADS
ROT_DIM = HEAD_DIM // 4
ROPE_THETA = 500000.0
N_IDX_HEADS = 8
IDX_DIM = 64
IDX_ROT = IDX_DIM // 4
TOPK_MAX = 256
Q_BLOCK = 128
N_KEYS = 128
N_EXPERTS = N_KEYS * N_KEYS
PEER_HEADS = 8
PEER_QDIM = 256
PEER_HALF = PEER_QDIM // 2
PEER_TOPK = 16
PEER_CHUNK = 128
EPS = 1e-6
IN_SPLITS = (D_POOL, D_ATTN, D_ATTN, D_ATTN, N_IDX_HEADS * IDX_DIM, IDX_DIM, N_IDX_HEADS)
D_IN = sum(IN_SPLITS)

kernel_name = "hybrid_pool_dsa_peer_adaln"


def rms_norm(x, g):
    xf = x.astype(jnp.float32)
    y = xf * lax.rsqrt(jnp.mean(xf * xf, axis=-1, keepdims=True) + EPS)
    return (y * g.astype(jnp.float32)).astype(x.dtype)


def rope_tables(positions, rot_dim):
    inv = ROPE_THETA ** (-jnp.arange(0, rot_dim, 2, dtype=jnp.float32) / rot_dim)
    ang = positions.astype(jnp.float32)[..., None] * inv
    return jnp.cos(ang), jnp.sin(ang)


def rope_partial(x, cos, sin):
    half = cos.shape[-1]
    rot = 2 * half
    xf = x.astype(jnp.float32)
    x1, x2, xp = xf[..., :half], xf[..., half:rot], xf[..., rot:]
    c, s = cos[:, :, None, :], sin[:, :, None, :]
    return jnp.concatenate([x1 * c - x2 * s, x1 * s + x2 * c, xp], axis=-1).astype(x.dtype)


def multiscale_pool(xp, w_pool, pool_scale):
    B, S, _ = xp.shape
    xf = xp.astype(jnp.float32).reshape(B, S, N_POOL_GROUPS, POOL_GROUP)
    cs = jnp.cumsum(xf, axis=1)
    t = jnp.arange(S)
    outs = []
    for g, w in enumerate(POOL_WINDOWS):
        cs_g = cs[:, :, g]
        lag = jnp.pad(cs_g, ((0, 0), (w, 0), (0, 0)))[:, :S]
        cnt = jnp.minimum(t + 1, w).astype(jnp.float32)[None, :, None]
        outs.append((cs_g - lag) / cnt - xf[:, :, g])
    pooled = jnp.stack(outs, axis=2).astype(xp.dtype)
    mixed = jnp.einsum('bsgc,gcd->bsgd', pooled, w_pool)
    return mixed.reshape(B, S, D_POOL) * pool_scale


def dsa_attention(q, k, v, q_idx, k_idx, w_idx, topk):
    B, S, H, hd = q.shape
    n_blocks = S // Q_BLOCK
    key_pos = jnp.arange(S)
    att_scale = HEAD_DIM ** -0.5
    kf = k_idx.astype(jnp.float32)

    def block(i):
        s0 = i * Q_BLOCK
        qb = lax.dynamic_slice_in_dim(q, s0, Q_BLOCK, axis=1)
        qib = lax.dynamic_slice_in_dim(q_idx, s0, Q_BLOCK, axis=1)
        wib = lax.dynamic_slice_in_dim(w_idx, s0, Q_BLOCK, axis=1)
        qpos = s0 + jnp.arange(Q_BLOCK)
        dots = jnp.einsum('bqhd,bsd->bqhs', qib.astype(jnp.float32), kf)
        score = jnp.einsum('bqhs,bqh->bqs', jax.nn.relu(dots), wib.astype(jnp.float32))
        causal = key_pos[None, :] <= qpos[:, None]
        score = jnp.where(causal[None], score, -jnp.inf)
        _, sel = lax.top_k(score, topk)
        kg = jax.vmap(lambda kk, ii: kk[ii])(k, sel)
        vg = jax.vmap(lambda vv, ii: vv[ii])(v, sel)
        logits = jnp.einsum('bqhd,bqkhd->bqhk', qb, kg).astype(jnp.float32) * att_scale
        valid = sel <= qpos[None, :, None]
        logits = jnp.where(valid[:, :, None, :], logits, -jnp.inf)
        p = jax.nn.softmax(logits, axis=-1).astype(v.dtype)
        return jnp.einsum('bqhk,bqkhd->bqhd', p, vg)

    out = lax.map(block, jnp.arange(n_blocks))
    return out.transpose(1, 0, 2, 3, 4).reshape(B, S, H * hd)


def peer_ffn(h, w_query, sub_keys, expert_u, expert_v):
    B, S, D = h.shape
    tokens = h.reshape(-1, PEER_CHUNK, D)

    def chunk(xc):
        T = xc.shape[0]
        q = (xc @ w_query).reshape(T, PEER_HEADS, 2, PEER_HALF)
        s = jnp.einsum('thpd,hpnd->thpn', q.astype(jnp.float32), sub_keys.astype(jnp.float32))
        sv, si = lax.top_k(s, PEER_TOPK)
        cand = (sv[:, :, 0, :, None] + sv[:, :, 1, None, :]).reshape(T, PEER_HEADS, -1)
        cand_idx = (si[:, :, 0, :, None] * N_KEYS + si[:, :, 1, None, :]).reshape(T, PEER_HEADS, -1)
        top_s, pos = lax.top_k(cand, PEER_TOPK)
        idx = jnp.take_along_axis(cand_idx, pos, axis=-1)
        g = jax.nn.softmax(top_s, axis=-1)
        u = expert_u[idx]
        v = expert_v[idx]
        a = jax.nn.gelu(jnp.einsum('td,thkd->thk', xc, u), approximate=False)
        return jnp.einsum('thk,thkd->td', (g * a).astype(v.dtype), v)

    return lax.map(chunk, tokens).reshape(B, S, D)


def setup_inputs(seed: int = 0) -> dict:
    key = jax.random.key(seed)
    ks = jax.random.split(key, 18)
    f32 = jnp.float32
    n = lambda k, shape, s: jax.random.normal(k, shape, f32) * s
    x = jax.random.normal(ks[0], (BATCH, SEQ, D_MODEL), f32)
    c = jax.random.normal(ks[1], (BATCH, D_MODEL), f32)
    offsets = jax.random.randint(ks[2], (BATCH, 1), 0, 4096, dtype=jnp.int32)
    positions = (offsets + jnp.arange(SEQ, dtype=jnp.int32)[None, :]).astype(jnp.int32)
    return {
        "x": x,
        "c": c,
        "positions": positions,
        "w_ada": n(ks[3], (DEPTH, D_MODEL, 6 * D_MODEL), 0.5 * D_MODEL ** -0.5),
        "b_ada": n(ks[4], (DEPTH, 6 * D_MODEL), 0.02),
        "norm_mix": 1.0 + n(ks[5], (DEPTH, D_MODEL), 0.02),
        "w_in": n(ks[6], (DEPTH, D_MODEL, D_IN), D_MODEL ** -0.5),
        "w_pool": n(ks[7], (DEPTH, N_POOL_GROUPS, POOL_GROUP, POOL_GROUP), POOL_GROUP ** -0.5),
        "pool_scale": 1.0 + n(ks[8], (DEPTH, D_POOL), 0.1),
        "w_out": n(ks[9], (DEPTH, D_MIX, D_MODEL), D_MIX ** -0.5),
        "norm_ffn": 1.0 + n(ks[10], (DEPTH, D_MODEL), 0.02),
        "w_query": n(ks[11], (DEPTH, D_MODEL, PEER_HEADS * PEER_QDIM), D_MODEL ** -0.5),
        "sub_keys": n(ks[12], (DEPTH, PEER_HEADS, 2, N_KEYS, PEER_HALF), PEER_HALF ** -0.5),
        "expert_u": n(ks[13], (DEPTH, N_EXPERTS, D_MODEL), D_MODEL ** -0.5),
        "expert_v": n(ks[14], (DEPTH, N_EXPERTS, D_MODEL), 0.5),
        "final_norm": 1.0 + n(ks[15], (D_MODEL,), 0.02),
    }


def reference(x, c, positions, w_ada, b_ada, norm_mix, w_in, w_pool, pool_scale, w_out,
              norm_ffn, w_query, sub_keys, expert_u, expert_v, final_norm):
    B, S, D = x.shape
    topk = min(TOPK_MAX, S // 4)
    cos, sin = rope_tables(positions, ROT_DIM)
    split_at = list(np.cumsum(IN_SPLITS)[:-1])
    for l in range(DEPTH):
        mod = (jax.nn.silu(c) @ w_ada[l] + b_ada[l])[:, None, :]
        sh1, sc1, g1, sh2, sc2, g2 = jnp.split(mod, 6, axis=-1)
        h = rms_norm(x, norm_mix[l]) * (1 + sc1) + sh1
        proj = h @ w_in[l]
        xp, q, k, v, q_idx, k_idx, w_idx = jnp.split(proj, split_at, axis=-1)
        pool_out = multiscale_pool(xp, w_pool[l], pool_scale[l])
        q = rope_partial(q.reshape(B, S, N_HEADS, HEAD_DIM), cos, sin)
        k = rope_partial(k.reshape(B, S, N_HEADS, HEAD_DIM), cos, sin)
        v = v.reshape(B, S, N_HEADS, HEAD_DIM)
        q_idx = rope_partial(q_idx.reshape(B, S, N_IDX_HEADS, IDX_DIM), cos, sin) * (IDX_DIM ** -0.5)
        k_idx = rope_partial(k_idx[:, :, None, :], cos, sin)[:, :, 0]
        w_idx = w_idx * (N_IDX_HEADS ** -0.5)
        attn_out = dsa_attention(q, k, v, q_idx, k_idx, w_idx, topk)
        mixed = jnp.concatenate([pool_out, attn_out], axis=-1)
        x = x + g1 * (mixed @ w_out[l])
        h = rms_norm(x, norm_ffn[l]) * (1 + sc2) + sh2
        x = x + g2 * peer_ffn(h, w_query[l], sub_keys[l], expert_u[l], expert_v[l])
    return rms_norm(x, final_norm)
```

```python
import functools

import jax
import jax.numpy as jnp
from jax import lax
from jax.experimental import pallas as pl
from jax.experimental.pallas import tpu as pltpu

F32 = jnp.float32
BF16 = jnp.bfloat16
I32 = jnp.int32

EPS = 1e-6
ROPE_THETA = 500000.0
N_HEADS = 8
HEAD_DIM = 64
ROT_HALF = 8
D_POOL = 512
D_ATTN = 512
POOL_WINDOWS = (2, 4, 8, 16)
POOL_GROUP = 128
POOL_HALO = 16
N_IDX_HEADS = 8
IDX_DIM = 64
TOPK_MAX = 256
N_KEYS = 128
PEER_HEADS = 8
PEER_TOPK = 16
LANES = 128
INT_MIN = -2147483648
NEG_BIG = -1e30
VMEM_LIMIT = 56 * 1024 * 1024

_NT = (((1,), (1,)), ((), ()))


def _params(sem):
    return pltpu.CompilerParams(dimension_semantics=sem, vmem_limit_bytes=VMEM_LIMIT)


def _mod_kernel(c_ref, w_ref, b_ref, o_ref):
    c = c_ref[...]
    s = c / (1.0 + jnp.exp(-c))
    o_ref[0] = jnp.dot(s, w_ref[0], preferred_element_type=F32,
                       precision=lax.Precision.HIGHEST) + b_ref[0]


def _modulation(c, w_ada, b_ada):
    depth, d, d6 = w_ada.shape
    b = c.shape[0]
    nj = d6 // d
    return pl.pallas_call(
        _mod_kernel,
        grid=(depth, nj),
        in_specs=[pl.BlockSpec((b, d), lambda l, j: (0, 0)),
                  pl.BlockSpec((1, d, d), lambda l, j: (l, 0, j)),
                  pl.BlockSpec((1, 1, d), lambda l, j: (l, 0, j))],
        out_specs=pl.BlockSpec((1, b, d), lambda l, j: (l, 0, j)),
        out_shape=jax.ShapeDtypeStruct((depth, b, d6), F32),
        compiler_params=_params(("parallel", "parallel")),
        name="adaln_mod",
    )(c, w_ada, b_ada.reshape(depth, 1, d6))


def _mod_norm(x, g, sc, sh):
    y = x * lax.rsqrt(jnp.mean(x * x, axis=-1, keepdims=True) + EPS)
    return (y * g) * (1.0 + sc) + sh


def _rope(z, c, s1, s2):
    n = z.shape[1] // LANES
    if n > 1:
        c = jnp.concatenate([c] * n, axis=1)
        s1 = jnp.concatenate([s1] * n, axis=1)
        s2 = jnp.concatenate([s2] * n, axis=1)
    w = z.shape[1]
    return z * c + pltpu.roll(z, ROT_HALF, 1) * s1 + pltpu.roll(z, w - ROT_HALF, 1) * s2


def _inproj_kernel(x_ref, sc_ref, sh_ref, g_ref, w_ref, c_ref, s1_ref, s2_ref,
                   xp_ref, qe_ref, qo_ref, k_ref, v_ref, qi_ref, kw_ref, ka_ref, kb_ref):
    h = _mod_norm(x_ref[...], g_ref[...], sc_ref[0], sh_ref[0]).astype(BF16)
    c, s1, s2 = c_ref[...], s1_ref[...], s2_ref[...]

    def proj(lo, hi):
        return jnp.dot(h, w_ref[:, lo:hi], preferred_element_type=F32)

    xp_ref[...] = proj(0, 512)
    lane = lax.broadcasted_iota(I32, (h.shape[0], D_ATTN), 1)
    even = (lane % LANES) < HEAD_DIM
    q = _rope(proj(512, 1024), c, s1, s2) * (HEAD_DIM ** -0.5)
    qe_ref[...] = jnp.where(even, q, 0.0).astype(BF16)
    qo_ref[...] = jnp.where(even, 0.0, q).astype(BF16)
    k_ref[...] = _rope(proj(1024, 1536), c, s1, s2).astype(BF16)
    v_ref[...] = proj(1536, 2048).astype(BF16)
    qi_ref[...] = (_rope(proj(2048, 2560), c, s1, s2) * (IDX_DIM ** -0.5)).astype(BF16)
    lane1 = lax.broadcasted_iota(I32, (h.shape[0], LANES), 1)
    is_k = lane1 < IDX_DIM
    kw = _rope(proj(2560, 2688), jnp.where(is_k, c, 1.0), jnp.where(is_k, s1, 0.0),
               jnp.where(is_k, s2, 0.0))
    kw = kw * jnp.where(is_k, 1.0, N_IDX_HEADS ** -0.5)
    kw_ref[...] = kw
    ka = jnp.where(is_k, kw, 0.0)
    ka_ref[...] = ka.astype(BF16)
    kb_ref[...] = pltpu.roll(ka, IDX_DIM, 1).astype(BF16)


def _in_projection(x2, sc, sh, g, w_pad, rc, rs1, rs2, seq, tm):
    n, d = x2.shape
    tps = seq // tm
    row = lambda w: pl.BlockSpec((tm, w), lambda i: (i, 0))
    per_b = pl.BlockSpec((1, 1, d), lambda i: (i // tps, 0, 0))
    out_w = [(512, F32)] + [(512, BF16)] * 5 + [(LANES, F32), (LANES, BF16), (LANES, BF16)]
    return pl.pallas_call(
        _inproj_kernel,
        grid=(n // tm,),
        in_specs=[row(d), per_b, per_b, pl.BlockSpec((1, d), lambda i: (0, 0)),
                  pl.BlockSpec(w_pad.shape, lambda i: (0, 0)),
                  row(LANES), row(LANES), row(LANES)],
        out_specs=[row(w) for w, _ in out_w],
        out_shape=[jax.ShapeDtypeStruct((n, w), dt) for w, dt in out_w],
        compiler_params=_params(("parallel",)),
        name="in_proj",
    )(x2, sc, sh, g, w_pad, rc, rs1, rs2)


def _select_kernel(qi_ref, kw_ref, ka_ref, kb_ref, bias_ref, keys_scr, wb_scr, *, tq, sc, topk):
    iq = pl.program_id(1)
    n_total = bias_ref.shape[1]
    nck = ((iq + 1) * tq + sc - 1) // sc
    kw = kw_ref[0]
    for h in range(N_IDX_HEADS):
        wb_scr[h] = jnp.broadcast_to(kw[:, IDX_DIM + h:IDX_DIM + h + 1], (tq, LANES))
    nl = sc // LANES
    row = iq * tq + lax.broadcasted_iota(I32, (tq, sc), 0)
    col0 = lax.broadcasted_iota(I32, (tq, sc), 1)

    def tile(a):
        return jnp.concatenate([a] * nl, axis=1) if nl > 1 else a

    def score_body(c, carry):
        off = pl.multiple_of(c * sc, sc)
        ka = ka_ref[0, pl.ds(off, sc), :]
        kb = kb_ref[0, pl.ds(off, sc), :]
        score = jnp.zeros((tq, sc), F32)
        for h in range(N_IDX_HEADS):
            qt = qi_ref[0, :, (h // 2) * LANES:(h // 2 + 1) * LANES]
            d = lax.dot_general(qt, ka if h % 2 == 0 else kb, _NT, preferred_element_type=F32)
            score = score + jnp.maximum(d, 0.0) * tile(wb_scr[h])
        bits = pltpu.bitcast(score, I32)
        key = bits ^ ((bits >> 31) & 0x7FFFFFFF)
        key = jnp.where(score == 0.0, 0, key)
        keys_scr[c] = jnp.where(col0 + off <= row, key, INT_MIN)
        return carry

    lax.fori_loop(0, nck, score_body, 0)

    def bit_body(i, th):
        cand = th + (jnp.int32(1) << (31 - i))

        def cnt_body(c, acc):
            k = keys_scr[c]
            for j in range(nl):
                acc = acc + jnp.where(k[:, j * LANES:(j + 1) * LANES] >= cand, 1.0, 0.0)
            return acc

        acc = lax.fori_loop(0, nck, cnt_body, jnp.zeros((tq, LANES), F32))
        cnt = jnp.sum(acc, axis=1, keepdims=True)
        return jnp.where(cnt >= float(topk), cand, th)

    th = lax.fori_loop(0, 32, bit_body, jnp.full((tq, LANES), INT_MIN, I32))
    th = tile(jnp.maximum(th, INT_MIN + 1))

    def out_body(c, carry):
        bias_ref[0, c] = jnp.where(keys_scr[c] >= th, 0.0, NEG_BIG).astype(BF16)
        return carry

    lax.fori_loop(0, nck, out_body, 0)

    def fill_body(c, carry):
        bias_ref[0, c] = jnp.full((tq, sc), NEG_BIG, BF16)
        return carry

    lax.fori_loop(nck, n_total, fill_body, 0)


def _select(qi, kw, ka, kb, topk, tq, sc):
    b, s, _ = qi.shape
    nq, nk = s // tq, s // sc
    kern = functools.partial(_select_kernel, tq=tq, sc=sc, topk=topk)
    return pl.pallas_call(
        kern,
        grid=(b, nq),
        in_specs=[pl.BlockSpec((1, tq, 512), lambda bi, i: (bi, i, 0)),
                  pl.BlockSpec((1, tq, LANES), lambda bi, i: (bi, i, 0)),
                  pl.BlockSpec((1, s, LANES), lambda bi, i: (bi, 0, 0)),
                  pl.BlockSpec((1, s, LANES), lambda bi, i: (bi, 0, 0))],
        out_specs=pl.BlockSpec((1, nk, tq, sc), lambda bi, i: (bi, 0, i, 0)),
        out_shape=jax.ShapeDtypeStruct((b, nk, s, sc), BF16),
        scratch_shapes=[pltpu.VMEM((nk, tq, sc), I32), pltpu.VMEM((N_IDX_HEADS, tq, LANES), F32)],
        compiler_params=_params(("parallel", "arbitrary")),
        name="dsa_select",
    )(qi, kw, ka, kb)


def _attn_kernel(qe_ref, qo_ref, k_ref, v_ref, bias_ref, o_ref, m_scr, l_scr, acc_scr, *, tq, sc):
    iq = pl.program_id(1)
    c = pl.program_id(2)
    nk = pl.num_programs(2)
    last = ((iq + 1) * tq - 1) // sc
    nl = sc // LANES
    lt = lax.broadcasted_iota(I32, (tq, LANES), 1) < HEAD_DIM

    @pl.when(c == 0)
    def _():
        m_scr[...] = jnp.full(m_scr.shape, -jnp.inf, F32)
        l_scr[...] = jnp.zeros(l_scr.shape, F32)
        acc_scr[...] = jnp.zeros(acc_scr.shape, F32)

    @pl.when(c <= last)
    def _():
        bias = bias_ref[0, 0].astype(F32)
        for j in range(N_HEADS // 2):
            sl = slice(j * LANES, (j + 1) * LANES)
            kp = k_ref[0, :, sl]
            vp = v_ref[0, :, sl]
            alphas, pvs = [], []
            for r, q_ref in enumerate((qe_ref, qo_ref)):
                h = 2 * j + r
                s = lax.dot_general(q_ref[0, :, sl], kp, _NT, preferred_element_type=F32) + bias
                m_prev = m_scr[h]
                m_new = jnp.maximum(m_prev, jnp.max(s, axis=1, keepdims=True))
                p = jnp.exp(s - (jnp.concatenate([m_new] * nl, axis=1) if nl > 1 else m_new))
                alpha = jnp.exp(m_prev - m_new)
                l_scr[h] = alpha * l_scr[h] + jnp.sum(p, axis=1, keepdims=True)
                m_scr[h] = m_new
                alphas.append(alpha)
                pvs.append(jnp.dot(p.astype(BF16), vp, preferred_element_type=F32))
            acc = acc_scr[:, sl]
            acc_scr[:, sl] = jnp.where(lt, alphas[0] * acc + pvs[0], alphas[1] * acc + pvs[1])

    @pl.when(c == nk - 1)
    def _():
        for j in range(N_HEADS // 2):
            sl = slice(j * LANES, (j + 1) * LANES)
            l = jnp.where(lt, l_scr[2 * j], l_scr[2 * j + 1])
            o_ref[0, :, sl] = (acc_scr[:, sl] / l).astype(BF16)


def _attention(qe, qo, k, v, bias, tq, sc):
    b, s, w = qe.shape
    nq, nk = s // tq, s // sc
    last = lambda i: ((i + 1) * tq - 1) // sc
    qspec = pl.BlockSpec((1, tq, w), lambda bi, i, c: (bi, i, 0))
    kspec = pl.BlockSpec((1, sc, w), lambda bi, i, c: (bi, jnp.minimum(c, last(i)), 0))
    kern = functools.partial(_attn_kernel, tq=tq, sc=sc)
    return pl.pallas_call(
        kern,
        grid=(b, nq, nk),
        in_specs=[qspec, qspec, kspec, kspec,
                  pl.BlockSpec((1, 1, tq, sc), lambda bi, i, c: (bi, jnp.minimum(c, last(i)), i, 0))],
        out_specs=pl.BlockSpec((1, tq, w), lambda bi, i, c: (bi, i, 0)),
        out_shape=jax.ShapeDtypeStruct((b, s, w), BF16),
        scratch_shapes=[pltpu.VMEM((N_HEADS, tq, LANES), F32), pltpu.VMEM((N_HEADS, tq, LANES), F32),
                        pltpu.VMEM((tq, w), F32)],
        compiler_params=_params(("parallel", "parallel", "arbitrary")),
        name="dsa_attention",
    )(qe, qo, k, v, bias)


def _outproj_kernel(xp_ref, halo_ref, attn_ref, x_ref, g1_ref, wpool_ref, ps_ref, wout_ref, o_ref,
                    ext_scr, *, tm, seq):
    i = pl.program_id(0)
    t0 = (i % (seq // tm)) * tm
    ext_scr[0:POOL_HALO, :] = halo_ref[...] * jnp.where(t0 == 0, 0.0, 1.0)
    ext_scr[POOL_HALO:POOL_HALO + tm, :] = xp_ref[...]
    t = t0 + lax.broadcasted_iota(I32, (tm, POOL_GROUP), 0)
    total = jnp.dot(attn_ref[...], wout_ref[D_POOL:D_POOL + D_ATTN, :], preferred_element_type=F32)
    for g, w in enumerate(POOL_WINDOWS):
        sl = slice(g * POOL_GROUP, (g + 1) * POOL_GROUP)
        cur = ext_scr[POOL_HALO:POOL_HALO + tm, sl]
        acc = cur
        for j in range(1, w):
            acc = acc + ext_scr[POOL_HALO - j:POOL_HALO - j + tm, sl]
        cnt = jnp.minimum(t + 1, w).astype(F32)
        pooled = acc / cnt - cur
        mixed = jnp.dot(pooled.astype(BF16), wpool_ref[g], preferred_element_type=F32) * ps_ref[:, sl]
        total = total + jnp.dot(mixed.astype(BF16), wout_ref[sl, :], preferred_element_type=F32)
    o_ref[...] = x_ref[...] + g1_ref[0] * total


def _out_projection(xp, attn, x2, g1, w_pool, pscale, w_out, seq, tm):
    n, d = x2.shape
    tps = seq // tm
    hb = tm // POOL_HALO
    row = lambda w: pl.BlockSpec((tm, w), lambda i: (i, 0))
    kern = functools.partial(_outproj_kernel, tm=tm, seq=seq)
    return pl.pallas_call(
        kern,
        grid=(n // tm,),
        in_specs=[row(D_POOL),
                  pl.BlockSpec((POOL_HALO, D_POOL), lambda i: (jnp.maximum(i * hb - 1, 0), 0)),
                  row(D_ATTN), row(d),
                  pl.BlockSpec((1, 1, d), lambda i: (i // tps, 0, 0)),
                  pl.BlockSpec(w_pool.shape, lambda i: (0, 0, 0)),
                  pl.BlockSpec((1, D_POOL), lambda i: (0, 0)),
                  pl.BlockSpec(w_out.shape, lambda i: (0, 0))],
        out_specs=row(d),
        out_shape=jax.ShapeDtypeStruct((n, d), F32),
        scratch_shapes=[pltpu.VMEM((tm + POOL_HALO, D_POOL), F32)],
        compiler_params=_params(("parallel",)),
        name="pool_out_proj",
    )(xp, xp, attn, x2, g1, w_pool, pscale, w_out)


def _top_values(w, n):
    outs = []
    for _ in range(n):
        mx = jnp.max(w, axis=0, keepdims=True)
        outs.append(mx)
        w = jnp.where(w == mx, -jnp.inf, w)
    return outs


def _router_kernel(x_ref, sc_ref, sh_ref, g_ref, wq_ref, sk_ref,
                   h_ref, s1_ref, scl_ref, s2_ref, e2_ref, th_ref):
    hb = _mod_norm(x_ref[...], g_ref[...], sc_ref[0], sh_ref[0]).astype(BF16)
    h_ref[...] = hb
    for hd in range(PEER_HEADS):
        s, tops = [], []
        for p in range(2):
            r0 = (hd * 2 + p) * N_KEYS
            qt = lax.dot_general(wq_ref[r0:r0 + N_KEYS, :], hb, _NT, preferred_element_type=F32)
            st = jnp.dot(sk_ref[hd, p], qt.astype(BF16), preferred_element_type=F32)
            s.append(st)
            tops.append(_top_values(st, PEER_TOPK))
        rank = lax.broadcasted_iota(I32, (PEER_TOPK, hb.shape[0]), 0)
        b_mat = jnp.zeros((PEER_TOPK, hb.shape[0]), F32)
        for r, brow in enumerate(tops[1]):
            b_mat = jnp.where(rank == r, brow, b_mat)
        cand = jnp.concatenate([a + b_mat for a in tops[0]], axis=0)
        best = _top_values(cand, PEER_TOPK)
        z = jnp.zeros_like(best[0])
        for cval in best:
            z = z + jnp.exp(cval - best[0])
        s1_ref[hd] = s[0]
        scl_ref[hd] = jnp.exp(s[0] - tops[0][0]) / z
        s2_ref[hd] = s[1]
        e2_ref[hd] = jnp.exp(s[1] - tops[1][0])
        th_ref[hd:hd + 1, :] = best[PEER_TOPK - 1]


def _router(x2, sc, sh, g, wq_t, sk, seq, tt):
    n, d = x2.shape
    tps = seq // tt
    per_b = pl.BlockSpec((1, 1, d), lambda i: (i // tps, 0, 0))
    big = pl.BlockSpec((PEER_HEADS, N_KEYS, tt), lambda i: (0, 0, i))
    big_shape = jax.ShapeDtypeStruct((PEER_HEADS, N_KEYS, n), F32)
    return pl.pallas_call(
        _router_kernel,
        grid=(n // tt,),
        in_specs=[pl.BlockSpec((tt, d), lambda i: (i, 0)), per_b, per_b,
                  pl.BlockSpec((1, d), lambda i: (0, 0)),
                  pl.BlockSpec(wq_t.shape, lambda i: (0, 0)),
                  pl.BlockSpec(sk.shape, lambda i: (0, 0, 0, 0))],
        out_specs=[pl.BlockSpec((tt, d), lambda i: (i, 0)), big, big, big, big,
                   pl.BlockSpec((PEER_HEADS, tt), lambda i: (0, i))],
        out_shape=[jax.ShapeDtypeStruct((n, d), BF16), big_shape, big_shape, big_shape, big_shape,
                   jax.ShapeDtypeStruct((PEER_HEADS, n), F32)],
        compiler_params=_params(("parallel",)),
        name="peer_router",
    )(x2, sc, sh, g, wq_t, sk)


def _expert_kernel(h_ref, u_ref, vt_ref, s1_ref, scl_ref, s2_ref, e2_ref, th_ref, x_ref, g2_ref,
                   o_ref, acc_scr, ct_scr, *, et):
    j = pl.program_id(1)
    nj = pl.num_programs(1)

    @pl.when(j == 0)
    def _():
        acc_scr[...] = jnp.zeros(acc_scr.shape, F32)

    at = lax.dot_general(u_ref[...], h_ref[...], _NT, preferred_element_type=F32)
    for k in range(et // N_KEYS):
        i1 = j * (et // N_KEYS) + k
        a = at[k * N_KEYS:(k + 1) * N_KEYS, :]
        w = jnp.zeros_like(a)
        for hd in range(PEER_HEADS):
            s1row = s1_ref[hd, pl.ds(i1, 1), :]
            sel = (s2_ref[hd] + s1row) >= th_ref[hd:hd + 1, :]
            w = w + jnp.where(sel, e2_ref[hd], 0.0) * scl_ref[hd, pl.ds(i1, 1), :]
        gelu = 0.5 * a * (1.0 + lax.erf(a * (2.0 ** -0.5)))
        ct_scr[k * N_KEYS:(k + 1) * N_KEYS, :] = (w * gelu).astype(BF16)
    acc_scr[...] += jnp.dot(vt_ref[...], ct_scr[...], preferred_element_type=F32)

    @pl.when(j == nj - 1)
    def _():
        o_ref[...] = x_ref[...] + g2_ref[0] * acc_scr[...].T


def _experts(hb, u, vt, s1, scl, s2, e2, th, x2, g2, seq, tt, et):
    n, d = x2.shape
    ne = u.shape[0]
    tps = seq // tt
    big = pl.BlockSpec((PEER_HEADS, N_KEYS, tt), lambda i, j: (0, 0, i))
    kern = functools.partial(_expert_kernel, et=et)
    return pl.pallas_call(
        kern,
        grid=(n // tt, ne // et),
        in_specs=[pl.BlockSpec((tt, d), lambda i, j: (i, 0)),
                  pl.BlockSpec((et, d), lambda i, j: (j, 0)),
                  pl.BlockSpec((d, et), lambda i, j: (0, j)),
                  big, big, big, big,
                  pl.BlockSpec((PEER_HEADS, tt), lambda i, j: (0, i)),
                  pl.BlockSpec((tt, d), lambda i, j: (i, 0)),
                  pl.BlockSpec((1, 1, d), lambda i, j: (i // tps, 0, 0))],
        out_specs=pl.BlockSpec((tt, d), lambda i, j: (i, 0)),
        out_shape=jax.ShapeDtypeStruct((n, d), F32),
        scratch_shapes=[pltpu.VMEM((d, tt), F32), pltpu.VMEM((et, tt), BF16)],
        compiler_params=_params(("parallel", "arbitrary")),
        name="peer_experts",
    )(hb, u, vt, s1, scl, s2, e2, th, x2, g2)


def _norm_kernel(x_ref, g_ref, o_ref):
    x = x_ref[...]
    o_ref[...] = x * lax.rsqrt(jnp.mean(x * x, axis=-1, keepdims=True) + EPS) * g_ref[...]


def _final_norm(x2, g, tm):
    n, d = x2.shape
    return pl.pallas_call(
        _norm_kernel,
        grid=(n // tm,),
        in_specs=[pl.BlockSpec((tm, d), lambda i: (i, 0)), pl.BlockSpec((1, d), lambda i: (0, 0))],
        out_specs=pl.BlockSpec((tm, d), lambda i: (i, 0)),
        out_shape=jax.ShapeDtypeStruct((n, d), F32),
        compiler_params=_params(("parallel",)),
        name="final_norm",
    )(x2, g)


def _rope_tables(positions):
    inv = ROPE_THETA ** (-jnp.arange(0, 2 * ROT_HALF, 2, dtype=F32) / (2 * ROT_HALF))
    ang = positions.astype(F32).reshape(-1, 1) * inv
    cos, sin = jnp.cos(ang), jnp.sin(ang)
    d = jnp.arange(LANES) % HEAD_DIM
    cosl, sinl = cos[:, d % ROT_HALF], sin[:, d % ROT_HALF]
    c = jnp.where(d < 2 * ROT_HALF, cosl, 1.0)
    s1 = jnp.where((d >= ROT_HALF) & (d < 2 * ROT_HALF), sinl, 0.0)
    s2 = jnp.where(d < ROT_HALF, -sinl, 0.0)
    return c, s1, s2


def _tile(dim, want):
    return min(dim, want)


def kernel(x, c, positions, w_ada, b_ada, norm_mix, w_in, w_pool, pool_scale, w_out, norm_ffn,
           w_query, sub_keys, expert_u, expert_v, final_norm):
    b, s, d = x.shape
    depth = w_ada.shape[0]
    n = b * s
    topk = min(TOPK_MAX, s // 4)
    tm = _tile(s, 512)
    tq = _tile(s, 256)
    sc = _tile(s, 512)
    tt = _tile(s, 512)
    et = 512

    rc, rs1, rs2 = _rope_tables(positions)
    mod = _modulation(c, w_ada, b_ada)
    d_in = w_in.shape[2]
    pad = (-d_in) % LANES
    x2 = x.reshape(n, d)
    for l in range(depth):
        sh1, sc1, g1, sh2, sc2, g2 = [mod[l, :, i * d:(i + 1) * d].reshape(b, 1, d) for i in range(6)]
        w_pad = jnp.pad(w_in[l], ((0, 0), (0, pad))).astype(BF16)
        xp, qe, qo, k, v, qi, kw, ka, kb = _in_projection(
            x2, sc1, sh1, norm_mix[l].reshape(1, d), w_pad, rc, rs1, rs2, s, tm)
        r3 = lambda a: a.reshape(b, s, a.shape[-1])
        bias = _select(r3(qi), r3(kw), r3(ka), r3(kb), topk, tq, sc)
        attn = _attention(r3(qe), r3(qo), r3(k), r3(v), bias, tq, sc).reshape(n, D_ATTN)
        x2 = _out_projection(xp, attn, x2, g1, w_pool[l].astype(BF16), pool_scale[l].reshape(1, D_POOL),
                             w_out[l].astype(BF16), s, tm)
        hb, s1, scl, s2, e2, th = _router(x2, sc2, sh2, norm_ffn[l].reshape(1, d),
                                          w_query[l].T.astype(BF16), sub_keys[l].astype(BF16), s, tt)
        x2 = _experts(hb, expert_u[l].astype(BF16), expert_v[l].T.astype(BF16), s1, scl, s2, e2, th,
                      x2, g2, s, tt, et)
    return _final_norm(x2, final_norm.reshape(1, d), tm).reshape(b, s, d)
```

```python
import functools

import jax
import jax.numpy as jnp
from jax import lax
from jax.experimental import pallas as pl
from jax.experimental.pallas import tpu as pltpu

F32 = jnp.float32
BF16 = jnp.bfloat16
I32 = jnp.int32

EPS = 1e-6
ROPE_THETA = 500000.0
N_HEADS = 8
HEAD_DIM = 64
ROT_HALF = 8
D_POOL = 512
D_ATTN = 512
POOL_WINDOWS = (2, 4, 8, 16)
POOL_GROUP = 128
POOL_HALO = 16
N_IDX_HEADS = 8
IDX_DIM = 64
TOPK_MAX = 256
N_KEYS = 128
PEER_HEADS = 8
PEER_TOPK = 16
LANES = 128
INT_MIN = -2147483648
LOG2E = 1.4426950408889634
NEG_BIG = -1e30
VMEM_LIMIT = 56 * 1024 * 1024

_NT = (((1,), (1,)), ((), ()))


def _params(sem):
    return pltpu.CompilerParams(dimension_semantics=sem, vmem_limit_bytes=VMEM_LIMIT)


def _mod_kernel(c_ref, w_ref, b_ref, o_ref):
    c = c_ref[...]
    s = c / (1.0 + jnp.exp(-c))
    o_ref[0] = jnp.dot(s, w_ref[0], preferred_element_type=F32,
                       precision=lax.Precision.HIGHEST) + b_ref[0]


def _modulation(c, w_ada, b_ada):
    depth, d, d6 = w_ada.shape
    b = c.shape[0]
    nj = d6 // d
    return pl.pallas_call(
        _mod_kernel,
        grid=(depth, nj),
        in_specs=[pl.BlockSpec((b, d), lambda l, j: (0, 0)),
                  pl.BlockSpec((1, d, d), lambda l, j: (l, 0, j)),
                  pl.BlockSpec((1, 1, d), lambda l, j: (l, 0, j))],
        out_specs=pl.BlockSpec((1, b, d), lambda l, j: (l, 0, j)),
        out_shape=jax.ShapeDtypeStruct((depth, b, d6), F32),
        compiler_params=_params(("parallel", "parallel")),
        name="adaln_mod",
    )(c, w_ada, b_ada.reshape(depth, 1, d6))


def _mod_norm(x, g, sc, sh):
    y = x * lax.rsqrt(jnp.mean(x * x, axis=-1, keepdims=True) + EPS)
    return (y * g) * (1.0 + sc) + sh


def _rope(z, c, s1, s2):
    n = z.shape[1] // LANES
    if n > 1:
        c = jnp.concatenate([c] * n, axis=1)
        s1 = jnp.concatenate([s1] * n, axis=1)
        s2 = jnp.concatenate([s2] * n, axis=1)
    w = z.shape[1]
    return z * c + pltpu.roll(z, ROT_HALF, 1) * s1 + pltpu.roll(z, w - ROT_HALF, 1) * s2


def _inproj_kernel(x_ref, sc_ref, sh_ref, g_ref, w_ref, c_ref, s1_ref, s2_ref,
                   xp_ref, qe_ref, qo_ref, k_ref, v_ref, qi_ref, kw_ref, ka_ref, kb_ref):
    h = _mod_norm(x_ref[...], g_ref[...], sc_ref[0], sh_ref[0]).astype(BF16)
    c, s1, s2 = c_ref[...], s1_ref[...], s2_ref[...]

    def proj(lo, hi):
        return jnp.dot(h, w_ref[:, lo:hi], preferred_element_type=F32)

    xp_ref[...] = proj(0, 512)
    lane = lax.broadcasted_iota(I32, (h.shape[0], D_ATTN), 1)
    even = (lane % LANES) < HEAD_DIM
    q = _rope(proj(512, 1024), c, s1, s2) * (HEAD_DIM ** -0.5 * LOG2E)
    qe_ref[...] = jnp.where(even, q, 0.0).astype(BF16)
    qo_ref[...] = jnp.where(even, 0.0, q).astype(BF16)
    k_ref[...] = _rope(proj(1024, 1536), c, s1, s2).astype(BF16)
    v_ref[...] = proj(1536, 2048).astype(BF16)
    qi_ref[...] = (_rope(proj(2048, 2560), c, s1, s2) * (IDX_DIM ** -0.5)).astype(BF16)
    lane1 = lax.broadcasted_iota(I32, (h.shape[0], LANES), 1)
    is_k = lane1 < IDX_DIM
    kw = _rope(proj(2560, 2688), jnp.where(is_k, c, 1.0), jnp.where(is_k, s1, 0.0),
               jnp.where(is_k, s2, 0.0))
    kw = kw * jnp.where(is_k, 1.0, N_IDX_HEADS ** -0.5)
    kw_ref[...] = kw
    ka = jnp.where(is_k, kw, 0.0)
    ka_ref[...] = ka.astype(BF16)
    kb_ref[...] = pltpu.roll(ka, IDX_DIM, 1).astype(BF16)


def _in_projection(x2, sc, sh, g, w_pad, rc, rs1, rs2, seq, tm):
    n, d = x2.shape
    tps = seq // tm
    row = lambda w: pl.BlockSpec((tm, w), lambda i: (i, 0))
    per_b = pl.BlockSpec((1, 1, d), lambda i: (i // tps, 0, 0))
    out_w = [(512, F32)] + [(512, BF16)] * 5 + [(LANES, F32), (LANES, BF16), (LANES, BF16)]
    return pl.pallas_call(
        _inproj_kernel,
        grid=(n // tm,),
        in_specs=[row(d), per_b, per_b, pl.BlockSpec((1, d), lambda i: (0, 0)),
                  pl.BlockSpec(w_pad.shape, lambda i: (0, 0)),
                  row(LANES), row(LANES), row(LANES)],
        out_specs=[row(w) for w, _ in out_w],
        out_shape=[jax.ShapeDtypeStruct((n, w), dt) for w, dt in out_w],
        compiler_params=_params(("parallel",)),
        name="in_proj",
    )(x2, sc, sh, g, w_pad, rc, rs1, rs2)


def _select_kernel(qi_ref, kw_ref, ka_ref, kb_ref, bias_ref, keys_scr, dig_scr, wb_scr, qs_scr, *,
                   tq, sc, topk):
    iq = pl.program_id(1)
    n_total = bias_ref.shape[1]
    nck = ((iq + 1) * tq + sc - 1) // sc
    kw = kw_ref[0]
    for h in range(N_IDX_HEADS):
        wb_scr[h] = jnp.broadcast_to(kw[:, IDX_DIM + h:IDX_DIM + h + 1], (tq, LANES))
    for j in range(N_IDX_HEADS // 2):
        qs_scr[j * tq:(j + 1) * tq, :] = qi_ref[0, :, j * LANES:(j + 1) * LANES]
    nl = sc // LANES
    row = iq * tq + lax.broadcasted_iota(I32, (tq, sc), 0)
    col0 = lax.broadcasted_iota(I32, (tq, sc), 1)

    def tile(a):
        return jnp.concatenate([a] * nl, axis=1) if nl > 1 else a

    def score_body(c, carry):
        off = pl.multiple_of(c * sc, sc)
        ka = ka_ref[0, pl.ds(off, sc), :]
        kb = kb_ref[0, pl.ds(off, sc), :]
        d_even = lax.dot_general(qs_scr[...], ka, _NT, preferred_element_type=F32)
        d_odd = lax.dot_general(qs_scr[...], kb, _NT, preferred_element_type=F32)
        score = jnp.zeros((tq, sc), F32)
        for j in range(N_IDX_HEADS // 2):
            rs = slice(j * tq, (j + 1) * tq)
            score = score + jnp.maximum(d_even[rs], 0.0) * tile(wb_scr[2 * j])
            score = score + jnp.maximum(d_odd[rs], 0.0) * tile(wb_scr[2 * j + 1])
        bits = pltpu.bitcast(score, I32)
        key = bits ^ ((bits >> 31) & 0x7FFFFFFF)
        key = jnp.where(score == 0.0, 0, key)
        keys_scr[c] = jnp.where(col0 + off <= row, key, INT_MIN)
        return carry

    lax.fori_loop(0, nck, score_body, 0)

    ones = jnp.ones((LANES, LANES), BF16)
    one_b, zero_b = jnp.ones((), BF16), jnp.zeros((), BF16)
    prefix = jnp.zeros((tq, LANES), I32)
    krem = jnp.full((tq, LANES), float(topk), F32)
    for p in range(4):
        sh = 24 - 8 * p

        def build_body(c, carry, sh=sh, p=p, prefix=prefix):
            u = keys_scr[c] ^ INT_MIN
            dig = (lax.shift_right_logical(u, sh) & 0xFF).astype(F32)
            if p > 0:
                dig = jnp.where(lax.shift_right_logical(u, sh + 8) == tile(prefix), dig, -1.0)
            dig_scr[c] = dig.astype(BF16)
            return carry

        lax.fori_loop(0, nck, build_body, 0)
        cur = jnp.zeros((tq, LANES), F32)
        ngt = jnp.zeros((tq, LANES), F32)
        for bit in (128, 64, 32, 16, 8, 4, 2, 1):
            cand = cur + float(bit)
            cand_b = cand.astype(BF16)

            def cnt_body(c, acc, cand_b=cand_b):
                dig = dig_scr[c]
                for j in range(nl):
                    acc = acc + jnp.where(dig[:, j * LANES:(j + 1) * LANES] >= cand_b, one_b, zero_b)
                return acc

            part = lax.fori_loop(0, nck, cnt_body, jnp.zeros((tq, LANES), BF16))
            cnt = jnp.dot(part, ones, preferred_element_type=F32)
            ok = cnt >= krem
            cur = jnp.where(ok, cand, cur)
            ngt = jnp.where(ok, ngt, cnt)
        krem = krem - ngt
        prefix = (prefix << 8) | cur.astype(I32)
    th = tile(jnp.maximum(prefix ^ INT_MIN, INT_MIN + 1))

    def out_body(c, carry):
        bias_ref[0, c] = jnp.where(keys_scr[c] >= th, 0.0, NEG_BIG).astype(BF16)
        return carry

    lax.fori_loop(0, nck, out_body, 0)

    def fill_body(c, carry):
        bias_ref[0, c] = jnp.full((tq, sc), NEG_BIG, BF16)
        return carry

    lax.fori_loop(nck, n_total, fill_body, 0)


def _select(qi, kw, ka, kb, topk, tq, sc):
    b, s, _ = qi.shape
    nq, nk = s // tq, s // sc
    kern = functools.partial(_select_kernel, tq=tq, sc=sc, topk=topk)
    return pl.pallas_call(
        kern,
        grid=(b, nq),
        in_specs=[pl.BlockSpec((1, tq, 512), lambda bi, i: (bi, i, 0)),
                  pl.BlockSpec((1, tq, LANES), lambda bi, i: (bi, i, 0)),
                  pl.BlockSpec((1, s, LANES), lambda bi, i: (bi, 0, 0)),
                  pl.BlockSpec((1, s, LANES), lambda bi, i: (bi, 0, 0))],
        out_specs=pl.BlockSpec((1, nk, tq, sc), lambda bi, i: (bi, 0, i, 0)),
        out_shape=jax.ShapeDtypeStruct((b, nk, s, sc), BF16),
        scratch_shapes=[pltpu.VMEM((nk, tq, sc), I32), pltpu.VMEM((nk, tq, sc), BF16),
                        pltpu.VMEM((N_IDX_HEADS, tq, LANES), F32),
                        pltpu.VMEM((N_IDX_HEADS // 2 * tq, LANES), BF16)],
        compiler_params=_params(("parallel", "arbitrary")),
        name="dsa_select",
    )(qi, kw, ka, kb)


def _attn_kernel(qe_ref, qo_ref, k_ref, v_ref, bias_ref, o_ref, m_scr, l_scr, acc_scr, *, tq, sc):
    iq = pl.program_id(1)
    c = pl.program_id(2)
    nk = pl.num_programs(2)
    last = ((iq + 1) * tq - 1) // sc
    nl = sc // LANES
    lt = lax.broadcasted_iota(I32, (tq, LANES), 1) < HEAD_DIM

    @pl.when(c == 0)
    def _():
        m_scr[...] = jnp.full(m_scr.shape, -jnp.inf, F32)
        l_scr[...] = jnp.zeros(l_scr.shape, F32)
        acc_scr[...] = jnp.zeros(acc_scr.shape, F32)

    @pl.when(c <= last)
    def _():
        bias = bias_ref[0, 0].astype(F32)
        bias2 = jnp.concatenate([bias, bias], axis=0)
        for j in range(N_HEADS // 2):
            sl = slice(j * LANES, (j + 1) * LANES)
            q2 = jnp.concatenate([qe_ref[0, :, sl], qo_ref[0, :, sl]], axis=0)
            s = lax.dot_general(q2, k_ref[0, :, sl], _NT, preferred_element_type=F32) + bias2
            m_prev = m_scr[j]
            m_new = jnp.maximum(m_prev, jnp.max(s, axis=1, keepdims=True))
            p = jnp.exp2(s - (jnp.concatenate([m_new] * nl, axis=1) if nl > 1 else m_new))
            alpha = jnp.exp2(m_prev - m_new)
            l_scr[j] = alpha * l_scr[j] + jnp.sum(p, axis=1, keepdims=True)
            m_scr[j] = m_new
            pv = jnp.dot(p.astype(BF16), v_ref[0, :, sl], preferred_element_type=F32)
            acc = acc_scr[:, sl]
            acc_scr[:, sl] = jnp.where(lt, alpha[:tq] * acc + pv[:tq], alpha[tq:] * acc + pv[tq:])

    @pl.when(c == nk - 1)
    def _():
        for j in range(N_HEADS // 2):
            sl = slice(j * LANES, (j + 1) * LANES)
            l = jnp.where(lt, l_scr[j, :tq], l_scr[j, tq:])
            o_ref[0, :, sl] = (acc_scr[:, sl] / l).astype(BF16)


def _attention(qe, qo, k, v, bias, tq, sc):
    b, s, w = qe.shape
    nq, nk = s // tq, s // sc
    last = lambda i: ((i + 1) * tq - 1) // sc
    qspec = pl.BlockSpec((1, tq, w), lambda bi, i, c: (bi, i, 0))
    kspec = pl.BlockSpec((1, sc, w), lambda bi, i, c: (bi, jnp.minimum(c, last(i)), 0))
    kern = functools.partial(_attn_kernel, tq=tq, sc=sc)
    return pl.pallas_call(
        kern,
        grid=(b, nq, nk),
        in_specs=[qspec, qspec, kspec, kspec,
                  pl.BlockSpec((1, 1, tq, sc), lambda bi, i, c: (bi, jnp.minimum(c, last(i)), i, 0))],
        out_specs=pl.BlockSpec((1, tq, w), lambda bi, i, c: (bi, i, 0)),
        out_shape=jax.ShapeDtypeStruct((b, s, w), BF16),
        scratch_shapes=[pltpu.VMEM((N_HEADS // 2, 2 * tq, LANES), F32),
                        pltpu.VMEM((N_HEADS // 2, 2 * tq, LANES), F32), pltpu.VMEM((tq, w), F32)],
        compiler_params=_params(("parallel", "parallel", "arbitrary")),
        name="dsa_attention",
    )(qe, qo, k, v, bias)


def _outproj_kernel(xp_ref, halo_ref, attn_ref, x_ref, g1_ref, wpool_ref, ps_ref, wout_ref, o_ref,
                    ext_scr, *, tm, seq):
    i = pl.program_id(0)
    t0 = (i % (seq // tm)) * tm
    ext_scr[0:POOL_HALO, :] = halo_ref[...] * jnp.where(t0 == 0, 0.0, 1.0)
    ext_scr[POOL_HALO:POOL_HALO + tm, :] = xp_ref[...]
    t = t0 + lax.broadcasted_iota(I32, (tm, POOL_GROUP), 0)
    total = jnp.dot(attn_ref[...], wout_ref[D_POOL:D_POOL + D_ATTN, :], preferred_element_type=F32)
    for g, w in enumerate(POOL_WINDOWS):
        sl = slice(g * POOL_GROUP, (g + 1) * POOL_GROUP)
        cur = ext_scr[POOL_HALO:POOL_HALO + tm, sl]
        acc = cur
        for j in range(1, w):
            acc = acc + ext_scr[POOL_HALO - j:POOL_HALO - j + tm, sl]
        cnt = jnp.minimum(t + 1, w).astype(F32)
        pooled = acc / cnt - cur
        mixed = jnp.dot(pooled.astype(BF16), wpool_ref[g], preferred_element_type=F32) * ps_ref[:, sl]
        total = total + jnp.dot(mixed.astype(BF16), wout_ref[sl, :], preferred_element_type=F32)
    o_ref[...] = x_ref[...] + g1_ref[0] * total


def _out_projection(xp, attn, x2, g1, w_pool, pscale, w_out, seq, tm):
    n, d = x2.shape
    tps = seq // tm
    hb = tm // POOL_HALO
    row = lambda w: pl.BlockSpec((tm, w), lambda i: (i, 0))
    kern = functools.partial(_outproj_kernel, tm=tm, seq=seq)
    return pl.pallas_call(
        kern,
        grid=(n // tm,),
        in_specs=[row(D_POOL),
                  pl.BlockSpec((POOL_HALO, D_POOL), lambda i: (jnp.maximum(i * hb - 1, 0), 0)),
                  row(D_ATTN), row(d),
                  pl.BlockSpec((1, 1, d), lambda i: (i // tps, 0, 0)),
                  pl.BlockSpec(w_pool.shape, lambda i: (0, 0, 0)),
                  pl.BlockSpec((1, D_POOL), lambda i: (0, 0)),
                  pl.BlockSpec(w_out.shape, lambda i: (0, 0))],
        out_specs=row(d),
        out_shape=jax.ShapeDtypeStruct((n, d), F32),
        scratch_shapes=[pltpu.VMEM((tm + POOL_HALO, D_POOL), F32)],
        compiler_params=_params(("parallel",)),
        name="pool_out_proj",
    )(xp, xp, attn, x2, g1, w_pool, pscale, w_out)


NO_RANK = 127.0


def _top_values(w, n, want_rank=False):
    outs = []
    rank = jnp.full(w.shape, NO_RANK, F32) if want_rank else None
    for r in range(n):
        mx = jnp.max(w, axis=0, keepdims=True)
        outs.append(mx)
        hit = w == mx
        if want_rank:
            rank = jnp.where(hit, float(r), rank)
        w = jnp.where(hit, -jnp.inf, w)
    return outs, rank


def _router_kernel(x_ref, sc_ref, sh_ref, g_ref, wq_ref, sk_ref,
                   h_ref, nrow_ref, scl_ref, rb_ref, e2_ref):
    hb = _mod_norm(x_ref[...], g_ref[...], sc_ref[0], sh_ref[0]).astype(BF16)
    h_ref[...] = hb
    tt = hb.shape[0]
    for hd in range(PEER_HEADS):
        s, tops, ranks = [], [], []
        for p in range(2):
            r0 = (hd * 2 + p) * N_KEYS
            qt = lax.dot_general(wq_ref[r0:r0 + N_KEYS, :], hb, _NT, preferred_element_type=F32)
            st = jnp.dot(sk_ref[hd, p], qt.astype(BF16), preferred_element_type=F32)
            s.append(st)
            top, rank = _top_values(st, PEER_TOPK, want_rank=True)
            tops.append(top)
            ranks.append(rank)
        row = lax.broadcasted_iota(I32, (PEER_TOPK, tt), 0)
        b_mat = jnp.zeros((PEER_TOPK, tt), F32)
        for r, brow in enumerate(tops[1]):
            b_mat = jnp.where(row == r, brow, b_mat)
        cands = [a + b_mat for a in tops[0]]
        best, _ = _top_values(jnp.concatenate(cands, axis=0), PEER_TOPK)
        theta = best[PEER_TOPK - 1]
        z = jnp.zeros_like(theta)
        for cval in best:
            z = z + jnp.exp(cval - best[0])
        nrow = jnp.zeros((N_KEYS, tt), F32)
        for r, cnd in enumerate(cands):
            n_r = jnp.sum(jnp.where(cnd >= theta, 1.0, 0.0), axis=0, keepdims=True)
            nrow = jnp.where(ranks[0] == float(r), n_r, nrow)
        nrow_ref[hd] = nrow
        scl_ref[hd] = jnp.exp(s[0] - tops[0][0]) / z
        rb_ref[hd] = ranks[1].astype(BF16)
        e2_ref[hd] = jnp.exp(s[1] - tops[1][0]).astype(BF16)


def _router(x2, sc, sh, g, wq_t, sk, seq, tt):
    n, d = x2.shape
    tps = seq // tt
    per_b = pl.BlockSpec((1, 1, d), lambda i: (i // tps, 0, 0))
    big = pl.BlockSpec((PEER_HEADS, N_KEYS, tt), lambda i: (0, 0, i))
    big_shape = lambda dt: jax.ShapeDtypeStruct((PEER_HEADS, N_KEYS, n), dt)
    return pl.pallas_call(
        _router_kernel,
        grid=(n // tt,),
        in_specs=[pl.BlockSpec((tt, d), lambda i: (i, 0)), per_b, per_b,
                  pl.BlockSpec((1, d), lambda i: (0, 0)),
                  pl.BlockSpec(wq_t.shape, lambda i: (0, 0)),
                  pl.BlockSpec(sk.shape, lambda i: (0, 0, 0, 0))],
        out_specs=[pl.BlockSpec((tt, d), lambda i: (i, 0)), big, big, big, big],
        out_shape=[jax.ShapeDtypeStruct((n, d), BF16), big_shape(F32), big_shape(F32),
                   big_shape(BF16), big_shape(BF16)],
        compiler_params=_params(("parallel",)),
        name="peer_router",
    )(x2, sc, sh, g, wq_t, sk)


def _expert_kernel(h_ref, u_ref, vt_ref, nrow_ref, scl_ref, rb_ref, e2_ref, x_ref, g2_ref,
                   o_ref, acc_scr, ct_scr, *, et):
    j = pl.program_id(1)
    nj = pl.num_programs(1)
    tt = h_ref.shape[0]

    @pl.when(j == 0)
    def _():
        acc_scr[...] = jnp.zeros(acc_scr.shape, F32)

    def rows(ref, hd, i1):
        r = jnp.broadcast_to(ref[hd, pl.ds(i1, 1), :], (16, tt)).astype(BF16)
        return jnp.concatenate([r] * (N_KEYS // 16), axis=0)

    at = lax.dot_general(u_ref[...], h_ref[...], _NT, preferred_element_type=F32)
    for k in range(et // N_KEYS):
        i1 = j * (et // N_KEYS) + k
        a = at[k * N_KEYS:(k + 1) * N_KEYS, :]
        w = jnp.zeros((N_KEYS, tt), BF16)
        for hd in range(PEER_HEADS):
            sel = rb_ref[hd] < rows(nrow_ref, hd, i1)
            w = w + jnp.where(sel, e2_ref[hd], jnp.zeros((), BF16)) * rows(scl_ref, hd, i1)
        gelu = 0.5 * a * (1.0 + lax.erf(a * (2.0 ** -0.5)))
        ct_scr[k * N_KEYS:(k + 1) * N_KEYS, :] = w * gelu.astype(BF16)
    acc_scr[...] += jnp.dot(vt_ref[...], ct_scr[...], preferred_element_type=F32)

    @pl.when(j == nj - 1)
    def _():
        o_ref[...] = x_ref[...] + g2_ref[0] * acc_scr[...].T


def _experts(hb, u, vt, nrow, scl, rb, e2, x2, g2, seq, tt, et):
    n, d = x2.shape
    ne = u.shape[0]
    tps = seq // tt
    big = pl.BlockSpec((PEER_HEADS, N_KEYS, tt), lambda i, j: (0, 0, i))
    kern = functools.partial(_expert_kernel, et=et)
    return pl.pallas_call(
        kern,
        grid=(n // tt, ne // et),
        in_specs=[pl.BlockSpec((tt, d), lambda i, j: (i, 0)),
                  pl.BlockSpec((et, d), lambda i, j: (j, 0)),
                  pl.BlockSpec((d, et), lambda i, j: (0, j)),
                  big, big, big, big,
                  pl.BlockSpec((tt, d), lambda i, j: (i, 0)),
                  pl.BlockSpec((1, 1, d), lambda i, j: (i // tps, 0, 0))],
        out_specs=pl.BlockSpec((tt, d), lambda i, j: (i, 0)),
        out_shape=jax.ShapeDtypeStruct((n, d), F32),
        scratch_shapes=[pltpu.VMEM((d, tt), F32), pltpu.VMEM((et, tt), BF16)],
        compiler_params=_params(("parallel", "arbitrary")),
        name="peer_experts",
    )(hb, u, vt, nrow, scl, rb, e2, x2, g2)


def _norm_kernel(x_ref, g_ref, o_ref):
    x = x_ref[...]
    o_ref[...] = x * lax.rsqrt(jnp.mean(x * x, axis=-1, keepdims=True) + EPS) * g_ref[...]


def _final_norm(x2, g, tm):
    n, d = x2.shape
    return pl.pallas_call(
        _norm_kernel,
        grid=(n // tm,),
        in_specs=[pl.BlockSpec((tm, d), lambda i: (i, 0)), pl.BlockSpec((1, d), lambda i: (0, 0))],
        out_specs=pl.BlockSpec((tm, d), lambda i: (i, 0)),
        out_shape=jax.ShapeDtypeStruct((n, d), F32),
        compiler_params=_params(("parallel",)),
        name="final_norm",
    )(x2, g)


def _rope_tables(positions):
    inv = ROPE_THETA ** (-jnp.arange(0, 2 * ROT_HALF, 2, dtype=F32) / (2 * ROT_HALF))
    ang = positions.astype(F32).reshape(-1, 1) * inv
    cos, sin = jnp.cos(ang), jnp.sin(ang)
    d = jnp.arange(LANES) % HEAD_DIM
    cosl, sinl = cos[:, d % ROT_HALF], sin[:, d % ROT_HALF]
    c = jnp.where(d < 2 * ROT_HALF, cosl, 1.0)
    s1 = jnp.where((d >= ROT_HALF) & (d < 2 * ROT_HALF), sinl, 0.0)
    s2 = jnp.where(d < ROT_HALF, -sinl, 0.0)
    return c, s1, s2


def _tile(dim, want):
    return min(dim, want)


def kernel(x, c, positions, w_ada, b_ada, norm_mix, w_in, w_pool, pool_scale, w_out, norm_ffn,
           w_query, sub_keys, expert_u, expert_v, final_norm):
    b, s, d = x.shape
    depth = w_ada.shape[0]
    n = b * s
    topk = min(TOPK_MAX, s // 4)
    tm = _tile(s, 512)
    tq = _tile(s, 256)
    sc = _tile(s, 512)
    tt = _tile(s, 512)
    et = 512

    rc, rs1, rs2 = _rope_tables(positions)
    mod = _modulation(c, w_ada, b_ada)
    d_in = w_in.shape[2]
    pad = (-d_in) % LANES
    x2 = x.reshape(n, d)
    for l in range(depth):
        sh1, sc1, g1, sh2, sc2, g2 = [mod[l, :, i * d:(i + 1) * d].reshape(b, 1, d) for i in range(6)]
        w_pad = jnp.pad(w_in[l], ((0, 0), (0, pad))).astype(BF16)
        xp, qe, qo, k, v, qi, kw, ka, kb = _in_projection(
            x2, sc1, sh1, norm_mix[l].reshape(1, d), w_pad, rc, rs1, rs2, s, tm)
        r3 = lambda a: a.reshape(b, s, a.shape[-1])
        bias = _select(r3(qi), r3(kw), r3(ka), r3(kb), topk, tq, sc)
        attn = _attention(r3(qe), r3(qo), r3(k), r3(v), bias, tq, sc).reshape(n, D_ATTN)
        x2 = _out_projection(xp, attn, x2, g1, w_pool[l].astype(BF16), pool_scale[l].reshape(1, D_POOL),
                             w_out[l].astype(BF16), s, tm)
        hb, nrow, scl, rb, e2 = _router(x2, sc2, sh2, norm_ffn[l].reshape(1, d),
                                        w_query[l].T.astype(BF16), sub_keys[l].astype(BF16), s, tt)
        x2 = _experts(hb, expert_u[l].astype(BF16), expert_v[l].T.astype(BF16), nrow, scl, rb, e2,
                      x2, g2, s, tt, et)
    return _final_norm(x2, final_norm.reshape(1, d), tm).reshape(b, s, d)
```

```python
import functools

import jax
import jax.numpy as jnp
from jax import lax
from jax.experimental import pallas as pl
from jax.experimental.pallas import tpu as pltpu

F32 = jnp.float32
BF16 = jnp.bfloat16
I32 = jnp.int32

EPS = 1e-6
ROPE_THETA = 500000.0
N_HEADS = 8
HEAD_DIM = 64
ROT_HALF = 8
D_POOL = 512
D_ATTN = 512
POOL_WINDOWS = (2, 4, 8, 16)
POOL_GROUP = 128
POOL_HALO = 16
N_IDX_HEADS = 8
IDX_DIM = 64
TOPK_MAX = 256
N_KEYS = 128
PEER_HEADS = 8
PEER_TOPK = 16
LANES = 128
INT_MIN = -2147483648
LOG2E = 1.4426950408889634
NEG_BIG = -1e30
VMEM_LIMIT = 56 * 1024 * 1024

_NT = (((1,), (1,)), ((), ()))


def _params(sem):
    return pltpu.CompilerParams(dimension_semantics=sem, vmem_limit_bytes=VMEM_LIMIT)


def _mod_kernel(c_ref, w_ref, b_ref, o_ref):
    c = c_ref[...]
    s = c / (1.0 + jnp.exp(-c))
    o_ref[0] = jnp.dot(s, w_ref[0], preferred_element_type=F32,
                       precision=lax.Precision.HIGHEST) + b_ref[0]


def _modulation(c, w_ada, b_ada):
    depth, d, d6 = w_ada.shape
    b = c.shape[0]
    nj = d6 // d
    return pl.pallas_call(
        _mod_kernel,
        grid=(depth, nj),
        in_specs=[pl.BlockSpec((b, d), lambda l, j: (0, 0)),
                  pl.BlockSpec((1, d, d), lambda l, j: (l, 0, j)),
                  pl.BlockSpec((1, 1, d), lambda l, j: (l, 0, j))],
        out_specs=pl.BlockSpec((1, b, d), lambda l, j: (l, 0, j)),
        out_shape=jax.ShapeDtypeStruct((depth, b, d6), F32),
        compiler_params=_params(("parallel", "parallel")),
        name="adaln_mod",
    )(c, w_ada, b_ada.reshape(depth, 1, d6))


def _mod_norm(x, g, sc, sh):
    y = x * lax.rsqrt(jnp.mean(x * x, axis=-1, keepdims=True) + EPS)
    return (y * g) * (1.0 + sc) + sh


def _rope(z, c, s1, s2):
    n = z.shape[1] // LANES
    if n > 1:
        c = jnp.concatenate([c] * n, axis=1)
        s1 = jnp.concatenate([s1] * n, axis=1)
        s2 = jnp.concatenate([s2] * n, axis=1)
    w = z.shape[1]
    return z * c + pltpu.roll(z, ROT_HALF, 1) * s1 + pltpu.roll(z, w - ROT_HALF, 1) * s2


def _inproj_kernel(x_ref, sc_ref, sh_ref, g_ref, w_ref, c_ref, s1_ref, s2_ref,
                   xp_ref, qe_ref, qo_ref, k_ref, v_ref, qi_ref, kw_ref, ka_ref, kb_ref):
    h = _mod_norm(x_ref[...], g_ref[...], sc_ref[0], sh_ref[0]).astype(BF16)
    c, s1, s2 = c_ref[...], s1_ref[...], s2_ref[...]

    def proj(lo, hi):
        return jnp.dot(h, w_ref[:, lo:hi], preferred_element_type=F32)

    xp_ref[...] = proj(0, 512)
    lane = lax.broadcasted_iota(I32, (h.shape[0], D_ATTN), 1)
    even = (lane % LANES) < HEAD_DIM
    q = _rope(proj(512, 1024), c, s1, s2) * (HEAD_DIM ** -0.5 * LOG2E)
    qe_ref[...] = jnp.where(even, q, 0.0).astype(BF16)
    qo_ref[...] = jnp.where(even, 0.0, q).astype(BF16)
    k_ref[...] = _rope(proj(1024, 1536), c, s1, s2).astype(BF16)
    v_ref[...] = proj(1536, 2048).astype(BF16)
    qi_ref[...] = (_rope(proj(2048, 2560), c, s1, s2) * (IDX_DIM ** -0.5)).astype(BF16)
    lane1 = lax.broadcasted_iota(I32, (h.shape[0], LANES), 1)
    is_k = lane1 < IDX_DIM
    kw = _rope(proj(2560, 2688), jnp.where(is_k, c, 1.0), jnp.where(is_k, s1, 0.0),
               jnp.where(is_k, s2, 0.0))
    kw = kw * jnp.where(is_k, 1.0, N_IDX_HEADS ** -0.5)
    kw_ref[...] = kw
    ka = jnp.where(is_k, kw, 0.0)
    ka_ref[...] = ka.astype(BF16)
    kb_ref[...] = pltpu.roll(ka, IDX_DIM, 1).astype(BF16)


def _in_projection(x2, sc, sh, g, w_pad, rc, rs1, rs2, seq, tm):
    n, d = x2.shape
    tps = seq // tm
    row = lambda w: pl.BlockSpec((tm, w), lambda i: (i, 0))
    per_b = pl.BlockSpec((1, 1, d), lambda i: (i // tps, 0, 0))
    out_w = [(512, F32)] + [(512, BF16)] * 5 + [(LANES, F32), (LANES, BF16), (LANES, BF16)]
    return pl.pallas_call(
        _inproj_kernel,
        grid=(n // tm,),
        in_specs=[row(d), per_b, per_b, pl.BlockSpec((1, d), lambda i: (0, 0)),
                  pl.BlockSpec(w_pad.shape, lambda i: (0, 0)),
                  row(LANES), row(LANES), row(LANES)],
        out_specs=[row(w) for w, _ in out_w],
        out_shape=[jax.ShapeDtypeStruct((n, w), dt) for w, dt in out_w],
        compiler_params=_params(("parallel",)),
        name="in_proj",
    )(x2, sc, sh, g, w_pad, rc, rs1, rs2)


def _select_kernel(qi_ref, kw_ref, ka_ref, kb_ref, bias_ref, keys_scr, wb_scr, qs_scr, *, tq, sc, topk):
    iq = pl.program_id(1)
    n_total = bias_ref.shape[1]
    nck = ((iq + 1) * tq + sc - 1) // sc
    kw = kw_ref[0]
    for h in range(N_IDX_HEADS):
        wb_scr[h] = jnp.broadcast_to(kw[:, IDX_DIM + h:IDX_DIM + h + 1], (tq, LANES))
    for j in range(N_IDX_HEADS // 2):
        qs_scr[j * tq:(j + 1) * tq, :] = qi_ref[0, :, j * LANES:(j + 1) * LANES]
    nl = sc // LANES
    row = iq * tq + lax.broadcasted_iota(I32, (tq, sc), 0)
    col0 = lax.broadcasted_iota(I32, (tq, sc), 1)

    def tile(a):
        return jnp.concatenate([a] * nl, axis=1) if nl > 1 else a

    def score_body(c, carry):
        off = pl.multiple_of(c * sc, sc)
        ka = ka_ref[0, pl.ds(off, sc), :]
        kb = kb_ref[0, pl.ds(off, sc), :]
        d_even = lax.dot_general(qs_scr[...], ka, _NT, preferred_element_type=F32)
        d_odd = lax.dot_general(qs_scr[...], kb, _NT, preferred_element_type=F32)
        score = jnp.zeros((tq, sc), F32)
        for j in range(N_IDX_HEADS // 2):
            rs = slice(j * tq, (j + 1) * tq)
            score = score + jnp.maximum(d_even[rs], 0.0) * tile(wb_scr[2 * j])
            score = score + jnp.maximum(d_odd[rs], 0.0) * tile(wb_scr[2 * j + 1])
        bits = pltpu.bitcast(score, I32)
        key = bits ^ ((bits >> 31) & 0x7FFFFFFF)
        key = jnp.where(score == 0.0, 0, key)
        keys_scr[c] = jnp.where(col0 + off <= row, key, INT_MIN)
        return carry

    lax.fori_loop(0, nck, score_body, 0)

    rb = min(tq, 64)

    def bit_body(i, th):
        cand = th + (jnp.int32(1) << (31 - i))
        parts = []
        for r0 in range(0, tq, rb):
            def cnt_body(c, acc, cb=cand[r0:r0 + rb], r0=r0):
                k = keys_scr[c, r0:r0 + rb, :]
                for j in range(nl):
                    acc = acc + jnp.where(k[:, j * LANES:(j + 1) * LANES] >= cb, 1.0, 0.0)
                return acc

            parts.append(lax.fori_loop(0, nck, cnt_body, jnp.zeros((rb, LANES), F32)))
        part = jnp.concatenate(parts, axis=0) if len(parts) > 1 else parts[0]
        cnt = jnp.sum(part, axis=1, keepdims=True)
        return jnp.where(cnt >= float(topk), cand, th)

    th = lax.fori_loop(0, 32, bit_body, jnp.full((tq, LANES), INT_MIN, I32))
    th = tile(jnp.maximum(th, INT_MIN + 1))

    def out_body(c, carry):
        bias_ref[0, c] = jnp.where(keys_scr[c] >= th, 0.0, NEG_BIG).astype(BF16)
        return carry

    lax.fori_loop(0, nck, out_body, 0)

    def fill_body(c, carry):
        bias_ref[0, c] = jnp.full((tq, sc), NEG_BIG, BF16)
        return carry

    lax.fori_loop(nck, n_total, fill_body, 0)


def _select(qi, kw, ka, kb, topk, tq, sc):
    b, s, _ = qi.shape
    nq, nk = s // tq, s // sc
    kern = functools.partial(_select_kernel, tq=tq, sc=sc, topk=topk)
    return pl.pallas_call(
        kern,
        grid=(b, nq),
        in_specs=[pl.BlockSpec((1, tq, 512), lambda bi, i: (bi, i, 0)),
                  pl.BlockSpec((1, tq, LANES), lambda bi, i: (bi, i, 0)),
                  pl.BlockSpec((1, s, LANES), lambda bi, i: (bi, 0, 0)),
                  pl.BlockSpec((1, s, LANES), lambda bi, i: (bi, 0, 0))],
        out_specs=pl.BlockSpec((1, nk, tq, sc), lambda bi, i: (bi, 0, i, 0)),
        out_shape=jax.ShapeDtypeStruct((b, nk, s, sc), BF16),
        scratch_shapes=[pltpu.VMEM((nk, tq, sc), I32), pltpu.VMEM((N_IDX_HEADS, tq, LANES), F32),
                        pltpu.VMEM((N_IDX_HEADS // 2 * tq, LANES), BF16)],
        compiler_params=_params(("parallel", "arbitrary")),
        name="dsa_select",
    )(qi, kw, ka, kb)


def _attn_kernel(qe_ref, qo_ref, k_ref, v_ref, bias_ref, o_ref, m_scr, l_scr, acc_scr, *, tq, sc):
    iq = pl.program_id(1)
    c = pl.program_id(2)
    nk = pl.num_programs(2)
    last = ((iq + 1) * tq - 1) // sc
    nl = sc // LANES
    lt = lax.broadcasted_iota(I32, (tq, LANES), 1) < HEAD_DIM

    @pl.when(c == 0)
    def _():
        m_scr[...] = jnp.full(m_scr.shape, -jnp.inf, F32)
        l_scr[...] = jnp.zeros(l_scr.shape, F32)
        acc_scr[...] = jnp.zeros(acc_scr.shape, F32)

    @pl.when(c <= last)
    def _():
        bias = bias_ref[0, 0].astype(F32)
        bias2 = jnp.concatenate([bias, bias], axis=0)
        for j in range(N_HEADS // 2):
            sl = slice(j * LANES, (j + 1) * LANES)
            q2 = jnp.concatenate([qe_ref[0, :, sl], qo_ref[0, :, sl]], axis=0)
            s = lax.dot_general(q2, k_ref[0, :, sl], _NT, preferred_element_type=F32) + bias2
            m_prev = m_scr[j]
            m_new = jnp.maximum(m_prev, jnp.max(s, axis=1, keepdims=True))
            p = jnp.exp2(s - (jnp.concatenate([m_new] * nl, axis=1) if nl > 1 else m_new))
            alpha = jnp.exp2(m_prev - m_new)
            l_scr[j] = alpha * l_scr[j] + jnp.sum(p, axis=1, keepdims=True)
            m_scr[j] = m_new
            pv = jnp.dot(p.astype(BF16), v_ref[0, :, sl], preferred_element_type=F32)
            acc = acc_scr[:, sl]
            acc_scr[:, sl] = jnp.where(lt, alpha[:tq] * acc + pv[:tq], alpha[tq:] * acc + pv[tq:])

    @pl.when(c == nk - 1)
    def _():
        for j in range(N_HEADS // 2):
            sl = slice(j * LANES, (j + 1) * LANES)
            l = jnp.where(lt, l_scr[j, :tq], l_scr[j, tq:])
            o_ref[0, :, sl] = (acc_scr[:, sl] / l).astype(BF16)


def _attention(qe, qo, k, v, bias, tq, sc):
    b, s, w = qe.shape
    nq, nk = s // tq, s // sc
    last = lambda i: ((i + 1) * tq - 1) // sc
    qspec = pl.BlockSpec((1, tq, w), lambda bi, i, c: (bi, i, 0))
    kspec = pl.BlockSpec((1, sc, w), lambda bi, i, c: (bi, jnp.minimum(c, last(i)), 0))
    kern = functools.partial(_attn_kernel, tq=tq, sc=sc)
    return pl.pallas_call(
        kern,
        grid=(b, nq, nk),
        in_specs=[qspec, qspec, kspec, kspec,
                  pl.BlockSpec((1, 1, tq, sc), lambda bi, i, c: (bi, jnp.minimum(c, last(i)), i, 0))],
        out_specs=pl.BlockSpec((1, tq, w), lambda bi, i, c: (bi, i, 0)),
        out_shape=jax.ShapeDtypeStruct((b, s, w), BF16),
        scratch_shapes=[pltpu.VMEM((N_HEADS // 2, 2 * tq, LANES), F32),
                        pltpu.VMEM((N_HEADS // 2, 2 * tq, LANES), F32), pltpu.VMEM((tq, w), F32)],
        compiler_params=_params(("parallel", "parallel", "arbitrary")),
        name="dsa_attention",
    )(qe, qo, k, v, bias)


def _outproj_kernel(xp_ref, halo_ref, attn_ref, x_ref, g1_ref, wpool_ref, ps_ref, wout_ref, o_ref,
                    ext_scr, *, tm, seq):
    i = pl.program_id(0)
    t0 = (i % (seq // tm)) * tm
    ext_scr[0:POOL_HALO, :] = halo_ref[...] * jnp.where(t0 == 0, 0.0, 1.0)
    ext_scr[POOL_HALO:POOL_HALO + tm, :] = xp_ref[...]
    t = t0 + lax.broadcasted_iota(I32, (tm, POOL_GROUP), 0)
    total = jnp.dot(attn_ref[...], wout_ref[D_POOL:D_POOL + D_ATTN, :], preferred_element_type=F32)
    for g, w in enumerate(POOL_WINDOWS):
        sl = slice(g * POOL_GROUP, (g + 1) * POOL_GROUP)
        cur = ext_scr[POOL_HALO:POOL_HALO + tm, sl]
        acc = cur
        for j in range(1, w):
            acc = acc + ext_scr[POOL_HALO - j:POOL_HALO - j + tm, sl]
        cnt = jnp.minimum(t + 1, w).astype(F32)
        pooled = acc / cnt - cur
        mixed = jnp.dot(pooled.astype(BF16), wpool_ref[g], preferred_element_type=F32) * ps_ref[:, sl]
        total = total + jnp.dot(mixed.astype(BF16), wout_ref[sl, :], preferred_element_type=F32)
    o_ref[...] = x_ref[...] + g1_ref[0] * total


def _out_projection(xp, attn, x2, g1, w_pool, pscale, w_out, seq, tm):
    n, d = x2.shape
    tps = seq // tm
    hb = tm // POOL_HALO
    row = lambda w: pl.BlockSpec((tm, w), lambda i: (i, 0))
    kern = functools.partial(_outproj_kernel, tm=tm, seq=seq)
    return pl.pallas_call(
        kern,
        grid=(n // tm,),
        in_specs=[row(D_POOL),
                  pl.BlockSpec((POOL_HALO, D_POOL), lambda i: (jnp.maximum(i * hb - 1, 0), 0)),
                  row(D_ATTN), row(d),
                  pl.BlockSpec((1, 1, d), lambda i: (i // tps, 0, 0)),
                  pl.BlockSpec(w_pool.shape, lambda i: (0, 0, 0)),
                  pl.BlockSpec((1, D_POOL), lambda i: (0, 0)),
                  pl.BlockSpec(w_out.shape, lambda i: (0, 0))],
        out_specs=row(d),
        out_shape=jax.ShapeDtypeStruct((n, d), F32),
        scratch_shapes=[pltpu.VMEM((tm + POOL_HALO, D_POOL), F32)],
        compiler_params=_params(("parallel",)),
        name="pool_out_proj",
    )(xp, xp, attn, x2, g1, w_pool, pscale, w_out)


EXPERT_SUB = 512
WDT = BF16
NO_RANK = 127.0


def _top_values(w, n, want_rank=False):
    outs = []
    rank = jnp.full(w.shape, NO_RANK, F32) if want_rank else None
    for r in range(n):
        mx = jnp.max(w, axis=0, keepdims=True)
        outs.append(mx)
        hit = w == mx
        if want_rank:
            rank = jnp.where(hit, float(r), rank)
        w = jnp.where(hit, -jnp.inf, w)
    return outs, rank


def _router_kernel(x_ref, sc_ref, sh_ref, g_ref, wq_ref, sk_ref,
                   h_ref, nrow_ref, scl_ref, rb_ref, e2_ref):
    hb = _mod_norm(x_ref[...], g_ref[...], sc_ref[0], sh_ref[0]).astype(BF16)
    h_ref[...] = hb
    tt = hb.shape[0]
    for hd in range(PEER_HEADS):
        s, tops, ranks = [], [], []
        for p in range(2):
            r0 = (hd * 2 + p) * N_KEYS
            qt = lax.dot_general(wq_ref[r0:r0 + N_KEYS, :], hb, _NT, preferred_element_type=F32)
            st = jnp.dot(sk_ref[hd, p], qt.astype(BF16), preferred_element_type=F32)
            s.append(st)
            top, rank = _top_values(st, PEER_TOPK, want_rank=True)
            tops.append(top)
            ranks.append(rank)
        row = lax.broadcasted_iota(I32, (PEER_TOPK, tt), 0)
        b_mat = jnp.zeros((PEER_TOPK, tt), F32)
        for r, brow in enumerate(tops[1]):
            b_mat = jnp.where(row == r, brow, b_mat)
        cands = [a + b_mat for a in tops[0]]
        best, _ = _top_values(jnp.concatenate(cands, axis=0), PEER_TOPK)
        theta = best[PEER_TOPK - 1]
        z = jnp.zeros_like(theta)
        for cval in best:
            z = z + jnp.exp(cval - best[0])
        nrow = jnp.zeros((N_KEYS, tt), F32)
        for r, cnd in enumerate(cands):
            n_r = jnp.sum(jnp.where(cnd >= theta, 1.0, 0.0), axis=0, keepdims=True)
            nrow = jnp.where(ranks[0] == float(r), n_r, nrow)
        nrow_ref[hd] = nrow
        scl_ref[hd] = jnp.exp(s[0] - tops[0][0]) / z
        rb_ref[hd] = ranks[1].astype(WDT)
        e2_ref[hd] = jnp.exp(s[1] - tops[1][0]).astype(WDT)


def _router(x2, sc, sh, g, wq_t, sk, seq, tt):
    n, d = x2.shape
    tps = seq // tt
    per_b = pl.BlockSpec((1, 1, d), lambda i: (i // tps, 0, 0))
    big = pl.BlockSpec((PEER_HEADS, N_KEYS, tt), lambda i: (0, 0, i))
    big_shape = lambda dt: jax.ShapeDtypeStruct((PEER_HEADS, N_KEYS, n), dt)
    return pl.pallas_call(
        _router_kernel,
        grid=(n // tt,),
        in_specs=[pl.BlockSpec((tt, d), lambda i: (i, 0)), per_b, per_b,
                  pl.BlockSpec((1, d), lambda i: (0, 0)),
                  pl.BlockSpec(wq_t.shape, lambda i: (0, 0)),
                  pl.BlockSpec(sk.shape, lambda i: (0, 0, 0, 0))],
        out_specs=[pl.BlockSpec((tt, d), lambda i: (i, 0)), big, big, big, big],
        out_shape=[jax.ShapeDtypeStruct((n, d), BF16), big_shape(F32), big_shape(F32),
                   big_shape(WDT), big_shape(WDT)],
        compiler_params=_params(("parallel",)),
        name="peer_router",
    )(x2, sc, sh, g, wq_t, sk)


def _expert_kernel(h_ref, u_ref, vt_ref, nrow_ref, scl_ref, rb_ref, e2_ref, x_ref, g2_ref,
                   o_ref, acc_scr, ct_scr, *, et):
    j = pl.program_id(1)
    nj = pl.num_programs(1)
    tt = h_ref.shape[0]

    @pl.when(j == 0)
    def _():
        acc_scr[...] = jnp.zeros(acc_scr.shape, F32)

    def rows(ref, hd, i1):
        r = jnp.broadcast_to(ref[hd, pl.ds(i1, 1), :], (16, tt)).astype(WDT)
        return jnp.concatenate([r] * (N_KEYS // 16), axis=0)

    for s0 in range(0, et, EXPERT_SUB):
        at = lax.dot_general(u_ref[s0:s0 + EXPERT_SUB, :], h_ref[...], _NT,
                             preferred_element_type=F32)
        for k in range(EXPERT_SUB // N_KEYS):
            i1 = (j * et + s0) // N_KEYS + k
            a = at[k * N_KEYS:(k + 1) * N_KEYS, :]
            w = jnp.zeros((N_KEYS, tt), WDT)
            for hd in range(PEER_HEADS):
                sel = rb_ref[hd] < rows(nrow_ref, hd, i1)
                w = w + jnp.where(sel, e2_ref[hd], jnp.zeros((), WDT)) * rows(scl_ref, hd, i1)
            gelu = 0.5 * a * (1.0 + lax.erf(a * (2.0 ** -0.5)))
            ct_scr[s0 + k * N_KEYS:s0 + (k + 1) * N_KEYS, :] = (w * gelu.astype(WDT)).astype(BF16)
        acc_scr[...] += jnp.dot(vt_ref[:, s0:s0 + EXPERT_SUB], ct_scr[s0:s0 + EXPERT_SUB, :],
                                preferred_element_type=F32)

    @pl.when(j == nj - 1)
    def _():
        o_ref[...] = x_ref[...] + g2_ref[0] * acc_scr[...].T


def _experts(hb, u, vt, nrow, scl, rb, e2, x2, g2, seq, tt, et):
    n, d = x2.shape
    ne = u.shape[0]
    tps = seq // tt
    big = pl.BlockSpec((PEER_HEADS, N_KEYS, tt), lambda i, j: (0, 0, i))
    kern = functools.partial(_expert_kernel, et=et)
    return pl.pallas_call(
        kern,
        grid=(n // tt, ne // et),
        in_specs=[pl.BlockSpec((tt, d), lambda i, j: (i, 0)),
                  pl.BlockSpec((et, d), lambda i, j: (j, 0)),
                  pl.BlockSpec((d, et), lambda i, j: (0, j)),
                  big, big, big, big,
                  pl.BlockSpec((tt, d), lambda i, j: (i, 0)),
                  pl.BlockSpec((1, 1, d), lambda i, j: (i // tps, 0, 0))],
        out_specs=pl.BlockSpec((tt, d), lambda i, j: (i, 0)),
        out_shape=jax.ShapeDtypeStruct((n, d), F32),
        scratch_shapes=[pltpu.VMEM((d, tt), F32), pltpu.VMEM((et, tt), BF16)],
        compiler_params=_params(("parallel", "arbitrary")),
        name="peer_experts",
    )(hb, u, vt, nrow, scl, rb, e2, x2, g2)


def _norm_kernel(x_ref, g_ref, o_ref):
    x = x_ref[...]
    o_ref[...] = x * lax.rsqrt(jnp.mean(x * x, axis=-1, keepdims=True) + EPS) * g_ref[...]


def _final_norm(x2, g, tm):
    n, d = x2.shape
    return pl.pallas_call(
        _norm_kernel,
        grid=(n // tm,),
        in_specs=[pl.BlockSpec((tm, d), lambda i: (i, 0)), pl.BlockSpec((1, d), lambda i: (0, 0))],
        out_specs=pl.BlockSpec((tm, d), lambda i: (i, 0)),
        out_shape=jax.ShapeDtypeStruct((n, d), F32),
        compiler_params=_params(("parallel",)),
        name="final_norm",
    )(x2, g)


def _rope_tables(positions):
    inv = ROPE_THETA ** (-jnp.arange(0, 2 * ROT_HALF, 2, dtype=F32) / (2 * ROT_HALF))
    ang = positions.astype(F32).reshape(-1, 1) * inv
    cos, sin = jnp.cos(ang), jnp.sin(ang)
    d = jnp.arange(LANES) % HEAD_DIM
    cosl, sinl = cos[:, d % ROT_HALF], sin[:, d % ROT_HALF]
    c = jnp.where(d < 2 * ROT_HALF, cosl, 1.0)
    s1 = jnp.where((d >= ROT_HALF) & (d < 2 * ROT_HALF), sinl, 0.0)
    s2 = jnp.where(d < ROT_HALF, -sinl, 0.0)
    return c, s1, s2


def _tile(dim, want):
    return min(dim, want)


def kernel(x, c, positions, w_ada, b_ada, norm_mix, w_in, w_pool, pool_scale, w_out, norm_ffn,
           w_query, sub_keys, expert_u, expert_v, final_norm):
    b, s, d = x.shape
    depth = w_ada.shape[0]
    n = b * s
    topk = min(TOPK_MAX, s // 4)
    tm = _tile(s, 512)
    tq = _tile(s, 512)
    sc = _tile(s, 512)
    tt = _tile(s, 512)
    et = 2 * EXPERT_SUB

    rc, rs1, rs2 = _rope_tables(positions)
    mod = _modulation(c, w_ada, b_ada)
    d_in = w_in.shape[2]
    pad = (-d_in) % LANES
    x2 = x.reshape(n, d)
    for l in range(depth):
        sh1, sc1, g1, sh2, sc2, g2 = [mod[l, :, i * d:(i + 1) * d].reshape(b, 1, d) for i in range(6)]
        w_pad = jnp.pad(w_in[l], ((0, 0), (0, pad))).astype(BF16)
        xp, qe, qo, k, v, qi, kw, ka, kb = _in_projection(
            x2, sc1, sh1, norm_mix[l].reshape(1, d), w_pad, rc, rs1, rs2, s, tm)
        r3 = lambda a: a.reshape(b, s, a.shape[-1])
        bias = _select(r3(qi), r3(kw), r3(ka), r3(kb), topk, tq, sc)
        attn = _attention(r3(qe), r3(qo), r3(k), r3(v), bias, tq, sc).reshape(n, D_ATTN)
        x2 = _out_projection(xp, attn, x2, g1, w_pool[l].astype(BF16), pool_scale[l].reshape(1, D_POOL),
                             w_out[l].astype(BF16), s, tm)
        hb, nrow, scl, rb, e2 = _router(x2, sc2, sh2, norm_ffn[l].reshape(1, d),
                                        w_query[l].T.astype(BF16), sub_keys[l].astype(BF16), s, tt)
        x2 = _experts(hb, expert_u[l].astype(BF16), expert_v[l].T.astype(BF16), nrow, scl, rb, e2,
                      x2, g2, s, tt, et)
    return _final_norm(x2, final_norm.reshape(1, d), tm).reshape(b, s, d)
```

```python
import functools

import jax
import jax.numpy as jnp
from jax import lax
from jax.experimental import pallas as pl
from jax.experimental.pallas import tpu as pltpu

F32 = jnp.float32
BF16 = jnp.bfloat16
I32 = jnp.int32

EPS = 1e-6
ROPE_THETA = 500000.0
N_HEADS = 8
HEAD_DIM = 64
ROT_HALF = 8
D_POOL = 512
D_ATTN = 512
POOL_WINDOWS = (2, 4, 8, 16)
POOL_GROUP = 128
POOL_HALO = 16
N_IDX_HEADS = 8
IDX_DIM = 64
TOPK_MAX = 256
N_KEYS = 128
PEER_HEADS = 8
PEER_TOPK = 16
LANES = 128
INT_MIN = -2147483648
LOG2E = 1.4426950408889634
NEG_BIG = -1e30
VMEM_LIMIT = 56 * 1024 * 1024

_NT = (((1,), (1,)), ((), ()))


def _params(sem):
    return pltpu.CompilerParams(dimension_semantics=sem, vmem_limit_bytes=VMEM_LIMIT)


def _mod_kernel(c_ref, w_ref, b_ref, o_ref):
    c = c_ref[...]
    s = c / (1.0 + jnp.exp(-c))
    o_ref[0] = jnp.dot(s, w_ref[0], preferred_element_type=F32,
                       precision=lax.Precision.HIGHEST) + b_ref[0]


def _modulation(c, w_ada, b_ada):
    depth, d, d6 = w_ada.shape
    b = c.shape[0]
    nj = d6 // d
    return pl.pallas_call(
        _mod_kernel,
        grid=(depth, nj),
        in_specs=[pl.BlockSpec((b, d), lambda l, j: (0, 0)),
                  pl.BlockSpec((1, d, d), lambda l, j: (l, 0, j)),
                  pl.BlockSpec((1, 1, d), lambda l, j: (l, 0, j))],
        out_specs=pl.BlockSpec((1, b, d), lambda l, j: (l, 0, j)),
        out_shape=jax.ShapeDtypeStruct((depth, b, d6), F32),
        compiler_params=_params(("parallel", "parallel")),
        name="adaln_mod",
    )(c, w_ada, b_ada.reshape(depth, 1, d6))


def _mod_norm(x, g, sc, sh):
    y = x * lax.rsqrt(jnp.mean(x * x, axis=-1, keepdims=True) + EPS)
    return (y * g) * (1.0 + sc) + sh


def _rope(z, c, s1, s2):
    n = z.shape[1] // LANES
    if n > 1:
        c = jnp.concatenate([c] * n, axis=1)
        s1 = jnp.concatenate([s1] * n, axis=1)
        s2 = jnp.concatenate([s2] * n, axis=1)
    w = z.shape[1]
    return z * c + pltpu.roll(z, ROT_HALF, 1) * s1 + pltpu.roll(z, w - ROT_HALF, 1) * s2


def _inproj_kernel(x_ref, sc_ref, sh_ref, g_ref, w_ref, c_ref, s1_ref, s2_ref,
                   xp_ref, qe_ref, qo_ref, k_ref, v_ref, qi_ref, kw_ref, ka_ref, kb_ref):
    h = _mod_norm(x_ref[...], g_ref[...], sc_ref[0], sh_ref[0]).astype(BF16)
    c, s1, s2 = c_ref[...], s1_ref[...], s2_ref[...]

    def proj(lo, hi):
        return jnp.dot(h, w_ref[:, lo:hi], preferred_element_type=F32)

    xp_ref[...] = proj(0, 512)
    lane = lax.broadcasted_iota(I32, (h.shape[0], D_ATTN), 1)
    even = (lane % LANES) < HEAD_DIM
    q = _rope(proj(512, 1024), c, s1, s2) * (HEAD_DIM ** -0.5 * LOG2E)
    qe_ref[...] = jnp.where(even, q, 0.0).astype(BF16)
    qo_ref[...] = jnp.where(even, 0.0, q).astype(BF16)
    k_ref[...] = _rope(proj(1024, 1536), c, s1, s2).astype(BF16)
    v_ref[...] = proj(1536, 2048).astype(BF16)
    qi_ref[...] = (_rope(proj(2048, 2560), c, s1, s2) * (IDX_DIM ** -0.5)).astype(BF16)
    lane1 = lax.broadcasted_iota(I32, (h.shape[0], LANES), 1)
    is_k = lane1 < IDX_DIM
    kw = _rope(proj(2560, 2688), jnp.where(is_k, c, 1.0), jnp.where(is_k, s1, 0.0),
               jnp.where(is_k, s2, 0.0))
    kw = kw * jnp.where(is_k, 1.0, N_IDX_HEADS ** -0.5)
    kw_ref[...] = kw
    ka = jnp.where(is_k, kw, 0.0)
    ka_ref[...] = ka.astype(BF16)
    kb_ref[...] = pltpu.roll(ka, IDX_DIM, 1).astype(BF16)


def _in_projection(x2, sc, sh, g, w_pad, rc, rs1, rs2, seq, tm):
    n, d = x2.shape
    tps = seq // tm
    row = lambda w: pl.BlockSpec((tm, w), lambda i: (i, 0))
    per_b = pl.BlockSpec((1, 1, d), lambda i: (i // tps, 0, 0))
    out_w = [(512, F32)] + [(512, BF16)] * 5 + [(LANES, F32), (LANES, BF16), (LANES, BF16)]
    return pl.pallas_call(
        _inproj_kernel,
        grid=(n // tm,),
        in_specs=[row(d), per_b, per_b, pl.BlockSpec((1, d), lambda i: (0, 0)),
                  pl.BlockSpec(w_pad.shape, lambda i: (0, 0)),
                  row(LANES), row(LANES), row(LANES)],
        out_specs=[row(w) for w, _ in out_w],
        out_shape=[jax.ShapeDtypeStruct((n, w), dt) for w, dt in out_w],
        compiler_params=_params(("parallel",)),
        name="in_proj",
    )(x2, sc, sh, g, w_pad, rc, rs1, rs2)


def _select_kernel(qi_ref, kw_ref, ka_ref, kb_ref, bias_ref, keys_scr, wb_scr, qs_scr, *, tq, sc, topk):
    iq = pl.program_id(1)
    n_total = bias_ref.shape[1]
    nck = ((iq + 1) * tq + sc - 1) // sc
    kw = kw_ref[0]
    for h in range(N_IDX_HEADS):
        wb_scr[h] = jnp.broadcast_to(kw[:, IDX_DIM + h:IDX_DIM + h + 1], (tq, LANES))
    for j in range(N_IDX_HEADS // 2):
        qs_scr[j * tq:(j + 1) * tq, :] = qi_ref[0, :, j * LANES:(j + 1) * LANES]
    nl = sc // LANES
    row = iq * tq + lax.broadcasted_iota(I32, (tq, sc), 0)
    col0 = lax.broadcasted_iota(I32, (tq, sc), 1)

    def tile(a):
        return jnp.concatenate([a] * nl, axis=1) if nl > 1 else a

    def score_body(c, carry):
        off = pl.multiple_of(c * sc, sc)
        ka = ka_ref[0, pl.ds(off, sc), :]
        kb = kb_ref[0, pl.ds(off, sc), :]
        d_even = lax.dot_general(qs_scr[...], ka, _NT, preferred_element_type=F32)
        d_odd = lax.dot_general(qs_scr[...], kb, _NT, preferred_element_type=F32)
        score = jnp.zeros((tq, sc), F32)
        for j in range(N_IDX_HEADS // 2):
            rs = slice(j * tq, (j + 1) * tq)
            score = score + jnp.maximum(d_even[rs], 0.0) * tile(wb_scr[2 * j])
            score = score + jnp.maximum(d_odd[rs], 0.0) * tile(wb_scr[2 * j + 1])
        bits = pltpu.bitcast(score, I32)
        key = bits ^ ((bits >> 31) & 0x7FFFFFFF)
        key = jnp.where(score == 0.0, 0, key)
        keys_scr[c] = jnp.where(col0 + off <= row, key, INT_MIN)
        return carry

    lax.fori_loop(0, nck, score_body, 0)

    rb = min(tq, 64)

    def bit_body(i, th):
        cand = th + (jnp.int32(1) << (31 - i))
        parts = []
        for r0 in range(0, tq, rb):
            def cnt_body(c, acc, cb=cand[r0:r0 + rb], r0=r0):
                k = keys_scr[c, r0:r0 + rb, :]
                for j in range(nl):
                    acc = acc + jnp.where(k[:, j * LANES:(j + 1) * LANES] >= cb, 1.0, 0.0)
                return acc

            parts.append(lax.fori_loop(0, nck, cnt_body, jnp.zeros((rb, LANES), F32)))
        part = jnp.concatenate(parts, axis=0) if len(parts) > 1 else parts[0]
        cnt = jnp.sum(part, axis=1, keepdims=True)
        return jnp.where(cnt >= float(topk), cand, th)

    th = lax.fori_loop(0, 32, bit_body, jnp.full((tq, LANES), INT_MIN, I32))
    th = tile(jnp.maximum(th, INT_MIN + 1))

    def out_body(c, carry):
        bias_ref[0, c] = jnp.where(keys_scr[c] >= th, 0.0, NEG_BIG).astype(BF16)
        return carry

    lax.fori_loop(0, nck, out_body, 0)

    def fill_body(c, carry):
        bias_ref[0, c] = jnp.full((tq, sc), NEG_BIG, BF16)
        return carry

    lax.fori_loop(nck, n_total, fill_body, 0)


def _select(qi, kw, ka, kb, topk, tq, sc):
    b, s, _ = qi.shape
    nq, nk = s // tq, s // sc
    kern = functools.partial(_select_kernel, tq=tq, sc=sc, topk=topk)
    return pl.pallas_call(
        kern,
        grid=(b, nq),
        in_specs=[pl.BlockSpec((1, tq, 512), lambda bi, i: (bi, i, 0)),
                  pl.BlockSpec((1, tq, LANES), lambda bi, i: (bi, i, 0)),
                  pl.BlockSpec((1, s, LANES), lambda bi, i: (bi, 0, 0)),
                  pl.BlockSpec((1, s, LANES), lambda bi, i: (bi, 0, 0))],
        out_specs=pl.BlockSpec((1, nk, tq, sc), lambda bi, i: (bi, 0, i, 0)),
        out_shape=jax.ShapeDtypeStruct((b, nk, s, sc), BF16),
        scratch_shapes=[pltpu.VMEM((nk, tq, sc), I32), pltpu.VMEM((N_IDX_HEADS, tq, LANES), F32),
                        pltpu.VMEM((N_IDX_HEADS // 2 * tq, LANES), BF16)],
        compiler_params=_params(("parallel", "arbitrary")),
        name="dsa_select",
    )(qi, kw, ka, kb)


def _attn_kernel(qe_ref, qo_ref, k_ref, v_ref, bias_ref, o_ref, m_scr, l_scr, acc_scr, *, tq, sc):
    iq = pl.program_id(1)
    c = pl.program_id(2)
    nk = pl.num_programs(2)
    last = ((iq + 1) * tq - 1) // sc
    nl = sc // LANES
    lt = lax.broadcasted_iota(I32, (tq, LANES), 1) < HEAD_DIM

    @pl.when(c == 0)
    def _():
        m_scr[...] = jnp.full(m_scr.shape, -jnp.inf, F32)
        l_scr[...] = jnp.zeros(l_scr.shape, F32)
        acc_scr[...] = jnp.zeros(acc_scr.shape, F32)

    @pl.when(c <= last)
    def _():
        bias = bias_ref[0, 0].astype(F32)
        bias2 = jnp.concatenate([bias, bias], axis=0)
        for j in range(N_HEADS // 2):
            sl = slice(j * LANES, (j + 1) * LANES)
            q2 = jnp.concatenate([qe_ref[0, :, sl], qo_ref[0, :, sl]], axis=0)
            s = lax.dot_general(q2, k_ref[0, :, sl], _NT, preferred_element_type=F32) + bias2
            m_prev = m_scr[j]
            m_new = jnp.maximum(m_prev, jnp.max(s, axis=1, keepdims=True))
            p = jnp.exp2(s - (jnp.concatenate([m_new] * nl, axis=1) if nl > 1 else m_new))
            alpha = jnp.exp2(m_prev - m_new)
            l_scr[j] = alpha * l_scr[j] + jnp.sum(p, axis=1, keepdims=True)
            m_scr[j] = m_new
            pv = jnp.dot(p.astype(BF16), v_ref[0, :, sl], preferred_element_type=F32)
            acc = acc_scr[:, sl]
            acc_scr[:, sl] = jnp.where(lt, alpha[:tq] * acc + pv[:tq], alpha[tq:] * acc + pv[tq:])

    @pl.when(c == nk - 1)
    def _():
        for j in range(N_HEADS // 2):
            sl = slice(j * LANES, (j + 1) * LANES)
            l = jnp.where(lt, l_scr[j, :tq], l_scr[j, tq:])
            o_ref[0, :, sl] = (acc_scr[:, sl] / l).astype(BF16)


def _attention(qe, qo, k, v, bias, tq, sc):
    b, s, w = qe.shape
    nq, nk = s // tq, s // sc
    last = lambda i: ((i + 1) * tq - 1) // sc
    qspec = pl.BlockSpec((1, tq, w), lambda bi, i, c: (bi, i, 0))
    kspec = pl.BlockSpec((1, sc, w), lambda bi, i, c: (bi, jnp.minimum(c, last(i)), 0))
    kern = functools.partial(_attn_kernel, tq=tq, sc=sc)
    return pl.pallas_call(
        kern,
        grid=(b, nq, nk),
        in_specs=[qspec, qspec, kspec, kspec,
                  pl.BlockSpec((1, 1, tq, sc), lambda bi, i, c: (bi, jnp.minimum(c, last(i)), i, 0))],
        out_specs=pl.BlockSpec((1, tq, w), lambda bi, i, c: (bi, i, 0)),
        out_shape=jax.ShapeDtypeStruct((b, s, w), BF16),
        scratch_shapes=[pltpu.VMEM((N_HEADS // 2, 2 * tq, LANES), F32),
                        pltpu.VMEM((N_HEADS // 2, 2 * tq, LANES), F32), pltpu.VMEM((tq, w), F32)],
        compiler_params=_params(("parallel", "parallel", "arbitrary")),
        name="dsa_attention",
    )(qe, qo, k, v, bias)


def _outproj_kernel(xp_ref, halo_ref, attn_ref, x_ref, g1_ref, wpool_ref, ps_ref, wout_ref, o_ref,
                    ext_scr, *, tm, seq):
    i = pl.program_id(0)
    t0 = (i % (seq // tm)) * tm
    ext_scr[0:POOL_HALO, :] = halo_ref[...] * jnp.where(t0 == 0, 0.0, 1.0)
    ext_scr[POOL_HALO:POOL_HALO + tm, :] = xp_ref[...]
    t = t0 + lax.broadcasted_iota(I32, (tm, POOL_GROUP), 0)
    total = jnp.dot(attn_ref[...], wout_ref[D_POOL:D_POOL + D_ATTN, :], preferred_element_type=F32)
    for g, w in enumerate(POOL_WINDOWS):
        sl = slice(g * POOL_GROUP, (g + 1) * POOL_GROUP)
        cur = ext_scr[POOL_HALO:POOL_HALO + tm, sl]
        acc = cur
        for j in range(1, w):
            acc = acc + ext_scr[POOL_HALO - j:POOL_HALO - j + tm, sl]
        cnt = jnp.minimum(t + 1, w).astype(F32)
        pooled = acc / cnt - cur
        mixed = jnp.dot(pooled.astype(BF16), wpool_ref[g], preferred_element_type=F32) * ps_ref[:, sl]
        total = total + jnp.dot(mixed.astype(BF16), wout_ref[sl, :], preferred_element_type=F32)
    o_ref[...] = x_ref[...] + g1_ref[0] * total


def _out_projection(xp, attn, x2, g1, w_pool, pscale, w_out, seq, tm):
    n, d = x2.shape
    tps = seq // tm
    hb = tm // POOL_HALO
    row = lambda w: pl.BlockSpec((tm, w), lambda i: (i, 0))
    kern = functools.partial(_outproj_kernel, tm=tm, seq=seq)
    return pl.pallas_call(
        kern,
        grid=(n // tm,),
        in_specs=[row(D_POOL),
                  pl.BlockSpec((POOL_HALO, D_POOL), lambda i: (jnp.maximum(i * hb - 1, 0), 0)),
                  row(D_ATTN), row(d),
                  pl.BlockSpec((1, 1, d), lambda i: (i // tps, 0, 0)),
                  pl.BlockSpec(w_pool.shape, lambda i: (0, 0, 0)),
                  pl.BlockSpec((1, D_POOL), lambda i: (0, 0)),
                  pl.BlockSpec(w_out.shape, lambda i: (0, 0))],
        out_specs=row(d),
        out_shape=jax.ShapeDtypeStruct((n, d), F32),
        scratch_shapes=[pltpu.VMEM((tm + POOL_HALO, D_POOL), F32)],
        compiler_params=_params(("parallel",)),
        name="pool_out_proj",
    )(xp, xp, attn, x2, g1, w_pool, pscale, w_out)


EXPERT_SUB = 512
WDT = BF16
NO_RANK = 127.0


def _top_values(w, n, want_rank=False):
    outs = []
    rank = jnp.full(w.shape, NO_RANK, F32) if want_rank else None
    for r in range(n):
        mx = jnp.max(w, axis=0, keepdims=True)
        outs.append(mx)
        hit = w == mx
        if want_rank:
            rank = jnp.where(hit, float(r), rank)
        w = jnp.where(hit, -jnp.inf, w)
    return outs, rank


def _router_kernel(x_ref, sc_ref, sh_ref, g_ref, wq_ref, sk_ref,
                   h_ref, nrow_ref, scl_ref, rb_ref, e2_ref):
    hb = _mod_norm(x_ref[...], g_ref[...], sc_ref[0], sh_ref[0]).astype(BF16)
    h_ref[...] = hb
    tt = hb.shape[0]
    for hd in range(PEER_HEADS):
        s, tops, ranks = [], [], []
        for p in range(2):
            r0 = (hd * 2 + p) * N_KEYS
            qt = lax.dot_general(wq_ref[r0:r0 + N_KEYS, :], hb, _NT, preferred_element_type=F32)
            st = jnp.dot(sk_ref[hd, p], qt.astype(BF16), preferred_element_type=F32)
            s.append(st)
            top, rank = _top_values(st, PEER_TOPK, want_rank=True)
            tops.append(top)
            ranks.append(rank)
        row = lax.broadcasted_iota(I32, (PEER_TOPK, tt), 0)
        b_mat = jnp.zeros((PEER_TOPK, tt), F32)
        for r, brow in enumerate(tops[1]):
            b_mat = jnp.where(row == r, brow, b_mat)
        cands = [a + b_mat for a in tops[0]]
        best, _ = _top_values(jnp.concatenate(cands, axis=0), PEER_TOPK)
        theta = best[PEER_TOPK - 1]
        z = jnp.zeros_like(theta)
        for cval in best:
            z = z + jnp.exp(cval - best[0])
        nrow = jnp.zeros((N_KEYS, tt), F32)
        for r, cnd in enumerate(cands):
            n_r = jnp.sum(jnp.where(cnd >= theta, 1.0, 0.0), axis=0, keepdims=True)
            nrow = jnp.where(ranks[0] == float(r), n_r, nrow)
        nrow_ref[hd] = nrow
        scl_ref[hd] = jnp.exp(s[0] - tops[0][0]) / z
        rb_ref[hd] = ranks[1].astype(WDT)
        e2_ref[hd] = jnp.exp(s[1] - tops[1][0]).astype(WDT)


def _router(x2, sc, sh, g, wq_t, sk, seq, tt):
    n, d = x2.shape
    tps = seq // tt
    per_b = pl.BlockSpec((1, 1, d), lambda i: (i // tps, 0, 0))
    big = pl.BlockSpec((PEER_HEADS, N_KEYS, tt), lambda i: (0, 0, i))
    big_shape = lambda dt: jax.ShapeDtypeStruct((PEER_HEADS, N_KEYS, n), dt)
    return pl.pallas_call(
        _router_kernel,
        grid=(n // tt,),
        in_specs=[pl.BlockSpec((tt, d), lambda i: (i, 0)), per_b, per_b,
                  pl.BlockSpec((1, d), lambda i: (0, 0)),
                  pl.BlockSpec(wq_t.shape, lambda i: (0, 0)),
                  pl.BlockSpec(sk.shape, lambda i: (0, 0, 0, 0))],
        out_specs=[pl.BlockSpec((tt, d), lambda i: (i, 0)), big, big, big, big],
        out_shape=[jax.ShapeDtypeStruct((n, d), BF16), big_shape(F32), big_shape(F32),
                   big_shape(WDT), big_shape(WDT)],
        compiler_params=_params(("parallel",)),
        name="peer_router",
    )(x2, sc, sh, g, wq_t, sk)


def _expert_kernel(h_ref, u_ref, vt_ref, nrow_ref, scl_ref, rb_ref, e2_ref, x_ref, g2_ref,
                   o_ref, acc_scr, ct_scr, *, et):
    j = pl.program_id(1)
    nj = pl.num_programs(1)
    tt = h_ref.shape[0]

    @pl.when(j == 0)
    def _():
        acc_scr[...] = jnp.zeros(acc_scr.shape, F32)

    def rows(ref, hd, i1):
        r = jnp.broadcast_to(ref[hd, pl.ds(i1, 1), :], (16, tt)).astype(WDT)
        return jnp.concatenate([r] * (N_KEYS // 16), axis=0)

    for s0 in range(0, et, EXPERT_SUB):
        at = lax.dot_general(u_ref[s0:s0 + EXPERT_SUB, :], h_ref[...], _NT,
                             preferred_element_type=F32)
        for k in range(EXPERT_SUB // N_KEYS):
            i1 = (j * et + s0) // N_KEYS + k
            a = at[k * N_KEYS:(k + 1) * N_KEYS, :]
            w = jnp.zeros((N_KEYS, tt), WDT)
            for hd in range(PEER_HEADS):
                sel = rb_ref[hd] < rows(nrow_ref, hd, i1)
                w = w + jnp.where(sel, e2_ref[hd], jnp.zeros((), WDT)) * rows(scl_ref, hd, i1)
            gelu = 0.5 * a * (1.0 + lax.erf(a * (2.0 ** -0.5)))
            ct_scr[s0 + k * N_KEYS:s0 + (k + 1) * N_KEYS, :] = (w * gelu.astype(WDT)).astype(BF16)
    acc_scr[...] += jnp.dot(vt_ref[...], ct_scr[...], preferred_element_type=F32)

    @pl.when(j == nj - 1)
    def _():
        o_ref[...] = x_ref[...] + g2_ref[0] * acc_scr[...].T


def _experts(hb, u, vt, nrow, scl, rb, e2, x2, g2, seq, tt, et):
    n, d = x2.shape
    ne = u.shape[0]
    tps = seq // tt
    big = pl.BlockSpec((PEER_HEADS, N_KEYS, tt), lambda i, j: (0, 0, i))
    kern = functools.partial(_expert_kernel, et=et)
    return pl.pallas_call(
        kern,
        grid=(n // tt, ne // et),
        in_specs=[pl.BlockSpec((tt, d), lambda i, j: (i, 0)),
                  pl.BlockSpec((et, d), lambda i, j: (j, 0)),
                  pl.BlockSpec((d, et), lambda i, j: (0, j)),
                  big, big, big, big,
                  pl.BlockSpec((tt, d), lambda i, j: (i, 0)),
                  pl.BlockSpec((1, 1, d), lambda i, j: (i // tps, 0, 0))],
        out_specs=pl.BlockSpec((tt, d), lambda i, j: (i, 0)),
        out_shape=jax.ShapeDtypeStruct((n, d), F32),
        scratch_shapes=[pltpu.VMEM((d, tt), F32), pltpu.VMEM((et, tt), BF16)],
        compiler_params=_params(("parallel", "arbitrary")),
        name="peer_experts",
    )(hb, u, vt, nrow, scl, rb, e2, x2, g2)


def _norm_kernel(x_ref, g_ref, o_ref):
    x = x_ref[...]
    o_ref[...] = x * lax.rsqrt(jnp.mean(x * x, axis=-1, keepdims=True) + EPS) * g_ref[...]


def _final_norm(x2, g, tm):
    n, d = x2.shape
    return pl.pallas_call(
        _norm_kernel,
        grid=(n // tm,),
        in_specs=[pl.BlockSpec((tm, d), lambda i: (i, 0)), pl.BlockSpec((1, d), lambda i: (0, 0))],
        out_specs=pl.BlockSpec((tm, d), lambda i: (i, 0)),
        out_shape=jax.ShapeDtypeStruct((n, d), F32),
        compiler_params=_params(("parallel",)),
        name="final_norm",
    )(x2, g)


def _rope_tables(positions):
    inv = ROPE_THETA ** (-jnp.arange(0, 2 * ROT_HALF, 2, dtype=F32) / (2 * ROT_HALF))
    ang = positions.astype(F32).reshape(-1, 1) * inv
    cos, sin = jnp.cos(ang), jnp.sin(ang)
    d = jnp.arange(LANES) % HEAD_DIM
    cosl, sinl = cos[:, d % ROT_HALF], sin[:, d % ROT_HALF]
    c = jnp.where(d < 2 * ROT_HALF, cosl, 1.0)
    s1 = jnp.where((d >= ROT_HALF) & (d < 2 * ROT_HALF), sinl, 0.0)
    s2 = jnp.where(d < ROT_HALF, -sinl, 0.0)
    return c, s1, s2


def _tile(dim, want):
    return min(dim, want)


def kernel(x, c, positions, w_ada, b_ada, norm_mix, w_in, w_pool, pool_scale, w_out, norm_ffn,
           w_query, sub_keys, expert_u, expert_v, final_norm):
    b, s, d = x.shape
    depth = w_ada.shape[0]
    n = b * s
    topk = min(TOPK_MAX, s // 4)
    tm = _tile(s, 512)
    tq = _tile(s, 512)
    sc = _tile(s, 512)
    tt = _tile(s, 512)
    et = 4 * EXPERT_SUB

    rc, rs1, rs2 = _rope_tables(positions)
    mod = _modulation(c, w_ada, b_ada)
    d_in = w_in.shape[2]
    pad = (-d_in) % LANES
    x2 = x.reshape(n, d)
    for l in range(depth):
        sh1, sc1, g1, sh2, sc2, g2 = [mod[l, :, i * d:(i + 1) * d].reshape(b, 1, d) for i in range(6)]
        w_pad = jnp.pad(w_in[l], ((0, 0), (0, pad))).astype(BF16)
        xp, qe, qo, k, v, qi, kw, ka, kb = _in_projection(
            x2, sc1, sh1, norm_mix[l].reshape(1, d), w_pad, rc, rs1, rs2, s, tm)
        r3 = lambda a: a.reshape(b, s, a.shape[-1])
        bias = _select(r3(qi), r3(kw), r3(ka), r3(kb), topk, tq, sc)
        attn = _attention(r3(qe), r3(qo), r3(k), r3(v), bias, tq, sc).reshape(n, D_ATTN)
        x2 = _out_projection(xp, attn, x2, g1, w_pool[l].astype(BF16), pool_scale[l].reshape(1, D_POOL),
                             w_out[l].astype(BF16), s, tm)
        hb, nrow, scl, rb, e2 = _router(x2, sc2, sh2, norm_ffn[l].reshape(1, d),
                                        w_query[l].T.astype(BF16), sub_keys[l].astype(BF16), s, tt)
        x2 = _experts(hb, expert_u[l].astype(BF16), expert_v[l].T.astype(BF16), nrow, scl, rb, e2,
                      x2, g2, s, tt, et)
    return _final_norm(x2, final_norm.reshape(1, d), tm).reshape(b, s, d)
```

```python
import functools

import jax
import jax.numpy as jnp
from jax import lax
from jax.experimental import pallas as pl
from jax.experimental.pallas import tpu as pltpu

F32 = jnp.float32
BF16 = jnp.bfloat16
I32 = jnp.int32

EPS = 1e-6
ROPE_THETA = 500000.0
N_HEADS = 8
HEAD_DIM = 64
ROT_HALF = 8
D_POOL = 512
D_ATTN = 512
POOL_WINDOWS = (2, 4, 8, 16)
POOL_GROUP = 128
POOL_HALO = 16
N_IDX_HEADS = 8
IDX_DIM = 64
TOPK_MAX = 256
N_KEYS = 128
PEER_HEADS = 8
PEER_TOPK = 16
LANES = 128
INT_MIN = -2147483648
LOG2E = 1.4426950408889634
NEG_BIG = -1e30
VMEM_LIMIT = 56 * 1024 * 1024

_NT = (((1,), (1,)), ((), ()))


def _params(sem):
    return pltpu.CompilerParams(dimension_semantics=sem, vmem_limit_bytes=VMEM_LIMIT)


def _mod_kernel(c_ref, w_ref, b_ref, o_ref):
    c = c_ref[...]
    s = c / (1.0 + jnp.exp(-c))
    o_ref[0] = jnp.dot(s, w_ref[0], preferred_element_type=F32,
                       precision=lax.Precision.HIGHEST) + b_ref[0]


def _modulation(c, w_ada, b_ada):
    depth, d, d6 = w_ada.shape
    b = c.shape[0]
    nj = d6 // d
    return pl.pallas_call(
        _mod_kernel,
        grid=(depth, nj),
        in_specs=[pl.BlockSpec((b, d), lambda l, j: (0, 0)),
                  pl.BlockSpec((1, d, d), lambda l, j: (l, 0, j)),
                  pl.BlockSpec((1, 1, d), lambda l, j: (l, 0, j))],
        out_specs=pl.BlockSpec((1, b, d), lambda l, j: (l, 0, j)),
        out_shape=jax.ShapeDtypeStruct((depth, b, d6), F32),
        compiler_params=_params(("parallel", "parallel")),
        name="adaln_mod",
    )(c, w_ada, b_ada.reshape(depth, 1, d6))


def _mod_norm(x, g, sc, sh):
    y = x * lax.rsqrt(jnp.mean(x * x, axis=-1, keepdims=True) + EPS)
    return (y * g) * (1.0 + sc) + sh


def _rope(z, c, s1, s2):
    n = z.shape[1] // LANES
    if n > 1:
        c = jnp.concatenate([c] * n, axis=1)
        s1 = jnp.concatenate([s1] * n, axis=1)
        s2 = jnp.concatenate([s2] * n, axis=1)
    w = z.shape[1]
    return z * c + pltpu.roll(z, ROT_HALF, 1) * s1 + pltpu.roll(z, w - ROT_HALF, 1) * s2


def _inproj_kernel(x_ref, sc_ref, sh_ref, g_ref, w_ref, c_ref, s1_ref, s2_ref,
                   xp_ref, qe_ref, qo_ref, k_ref, v_ref, qi_ref, kw_ref, ka_ref, kb_ref):
    h = _mod_norm(x_ref[...], g_ref[...], sc_ref[0], sh_ref[0]).astype(BF16)
    c, s1, s2 = c_ref[...], s1_ref[...], s2_ref[...]

    def proj(lo, hi):
        return jnp.dot(h, w_ref[:, lo:hi], preferred_element_type=F32)

    xp_ref[...] = proj(0, 512)
    lane = lax.broadcasted_iota(I32, (h.shape[0], D_ATTN), 1)
    even = (lane % LANES) < HEAD_DIM
    q = _rope(proj(512, 1024), c, s1, s2) * (HEAD_DIM ** -0.5 * LOG2E)
    qe_ref[...] = jnp.where(even, q, 0.0).astype(BF16)
    qo_ref[...] = jnp.where(even, 0.0, q).astype(BF16)
    k_ref[...] = _rope(proj(1024, 1536), c, s1, s2).astype(BF16)
    v_ref[...] = proj(1536, 2048).astype(BF16)
    qi_ref[...] = (_rope(proj(2048, 2560), c, s1, s2) * (IDX_DIM ** -0.5)).astype(BF16)
    lane1 = lax.broadcasted_iota(I32, (h.shape[0], LANES), 1)
    is_k = lane1 < IDX_DIM
    kw = _rope(proj(2560, 2688), jnp.where(is_k, c, 1.0), jnp.where(is_k, s1, 0.0),
               jnp.where(is_k, s2, 0.0))
    kw = kw * jnp.where(is_k, 1.0, N_IDX_HEADS ** -0.5)
    kw_ref[...] = kw
    ka = jnp.where(is_k, kw, 0.0)
    ka_ref[...] = ka.astype(BF16)
    kb_ref[...] = pltpu.roll(ka, IDX_DIM, 1).astype(BF16)


def _in_projection(x2, sc, sh, g, w_pad, rc, rs1, rs2, seq, tm):
    n, d = x2.shape
    tps = seq // tm
    row = lambda w: pl.BlockSpec((tm, w), lambda i: (i, 0))
    per_b = pl.BlockSpec((1, 1, d), lambda i: (i // tps, 0, 0))
    out_w = [(512, F32)] + [(512, BF16)] * 5 + [(LANES, F32), (LANES, BF16), (LANES, BF16)]
    return pl.pallas_call(
        _inproj_kernel,
        grid=(n // tm,),
        in_specs=[row(d), per_b, per_b, pl.BlockSpec((1, d), lambda i: (0, 0)),
                  pl.BlockSpec(w_pad.shape, lambda i: (0, 0)),
                  row(LANES), row(LANES), row(LANES)],
        out_specs=[row(w) for w, _ in out_w],
        out_shape=[jax.ShapeDtypeStruct((n, w), dt) for w, dt in out_w],
        compiler_params=_params(("parallel",)),
        name="in_proj",
    )(x2, sc, sh, g, w_pad, rc, rs1, rs2)


def _select_kernel(qi_ref, kw_ref, ka_ref, kb_ref, bias_ref, keys_scr, wb_scr, qs_scr, cut_scr, *,
                   tq, sc, topk):
    iq = pl.program_id(1)
    n_total = bias_ref.shape[1]
    nck = ((iq + 1) * tq + sc - 1) // sc
    kw = kw_ref[0]
    for h in range(N_IDX_HEADS):
        wb_scr[h] = jnp.broadcast_to(kw[:, IDX_DIM + h:IDX_DIM + h + 1], (tq, LANES))
    for j in range(N_IDX_HEADS // 2):
        qs_scr[j * tq:(j + 1) * tq, :] = qi_ref[0, :, j * LANES:(j + 1) * LANES]
    nl = sc // LANES
    row = iq * tq + lax.broadcasted_iota(I32, (tq, sc), 0)
    col0 = lax.broadcasted_iota(I32, (tq, sc), 1)

    def tile(a):
        return jnp.concatenate([a] * nl, axis=1) if nl > 1 else a

    def score_body(c, carry):
        off = pl.multiple_of(c * sc, sc)
        ka = ka_ref[0, pl.ds(off, sc), :]
        kb = kb_ref[0, pl.ds(off, sc), :]
        d_even = lax.dot_general(qs_scr[...], ka, _NT, preferred_element_type=F32)
        d_odd = lax.dot_general(qs_scr[...], kb, _NT, preferred_element_type=F32)
        score = jnp.zeros((tq, sc), F32)
        for j in range(N_IDX_HEADS // 2):
            rs = slice(j * tq, (j + 1) * tq)
            score = score + jnp.maximum(d_even[rs], 0.0) * tile(wb_scr[2 * j])
            score = score + jnp.maximum(d_odd[rs], 0.0) * tile(wb_scr[2 * j + 1])
        bits = pltpu.bitcast(score, I32)
        key = bits ^ ((bits >> 31) & 0x7FFFFFFF)
        key = jnp.where(score == 0.0, 0, key)
        keys_scr[c] = jnp.where(col0 + off <= row, key, INT_MIN)
        return carry

    lax.fori_loop(0, nck, score_body, 0)

    rb = min(tq, 64)
    rb_tie = min(tq, 32)
    lane = lax.broadcasted_iota(I32, (rb_tie, LANES), 1)

    def count(hit, *row_args, rows=rb, gates=None):
        parts = []
        for b, r0 in enumerate(range(0, tq, rows)):
            blk = [a[r0:r0 + rows] for a in row_args]

            def cnt_body(c, acc, r0=r0, blk=blk):
                k = keys_scr[c, r0:r0 + rows, :]
                for j in range(nl):
                    acc = acc + hit(k[:, j * LANES:(j + 1) * LANES], c * sc + j * LANES, *blk)
                return acc

            zero = jnp.zeros((rows, LANES), F32)
            scan = functools.partial(lax.fori_loop, 0, nck, cnt_body, zero)
            parts.append(scan() if gates is None else lax.cond(gates[b], scan, lambda zero=zero: zero))
        part = jnp.concatenate(parts, axis=0) if len(parts) > 1 else parts[0]
        return jnp.sum(part, axis=1, keepdims=True)

    def bit_body(i, carry):
        th, n_ge = carry
        cand = th + (jnp.int32(1) << (31 - i))
        cnt = count(lambda k, off, cb: jnp.where(k >= cb, 1.0, 0.0), cand)
        ok = cnt >= float(topk)
        return jnp.where(ok, cand, th), jnp.where(ok, cnt, n_ge)

    th, n_ge = lax.fori_loop(0, 32, bit_body, (jnp.full((tq, LANES), INT_MIN, I32),
                                               jnp.zeros((tq, LANES), F32)))
    excess = jnp.logical_and(n_ge > float(topk), th != INT_MIN)
    cut_scr[...] = jnp.full((tq, LANES), n_total * sc, I32)

    excess_f = jnp.where(excess, 1.0, 0.0)

    @pl.when(jnp.max(excess_f) > 0.0)
    def _():
        gates = [jnp.max(excess_f[r0:r0 + rb_tie]) > 0.0 for r0 in range(0, tq, rb_tie)]
        n_gt = count(lambda k, off, tb: jnp.where(k > tb, 1.0, 0.0), th, rows=rb_tie, gates=gates)
        need = float(topk) - n_gt

        def cut_body(i, cut):
            cand = cut + (jnp.int32(n_total * sc // 2) >> i)
            before = count(lambda k, off, tb, cb: jnp.where(k == tb, jnp.where(lane + off < cb, 1.0, 0.0), 0.0),
                           th, cand, rows=rb_tie, gates=gates)
            return jnp.where(before < need, cand, cut)

        n_bits = (n_total * sc).bit_length() - 1
        cut = lax.fori_loop(0, n_bits, cut_body, jnp.zeros((tq, LANES), I32))
        cut_scr[...] = jnp.where(excess, cut, n_total * sc)

    th = tile(jnp.maximum(th, INT_MIN + 1))
    cut = tile(cut_scr[...])

    def out_body(c, carry):
        k = keys_scr[c]
        tie = jnp.where(k == th, jnp.where(col0 + c * sc <= cut, 0.0, NEG_BIG), NEG_BIG)
        bias_ref[0, c] = jnp.where(k > th, 0.0, tie).astype(BF16)
        return carry

    lax.fori_loop(0, nck, out_body, 0)

    def fill_body(c, carry):
        bias_ref[0, c] = jnp.full((tq, sc), NEG_BIG, BF16)
        return carry

    lax.fori_loop(nck, n_total, fill_body, 0)


def _select(qi, kw, ka, kb, topk, tq, sc):
    b, s, _ = qi.shape
    nq, nk = s // tq, s // sc
    kern = functools.partial(_select_kernel, tq=tq, sc=sc, topk=topk)
    return pl.pallas_call(
        kern,
        grid=(b, nq),
        in_specs=[pl.BlockSpec((1, tq, 512), lambda bi, i: (bi, i, 0)),
                  pl.BlockSpec((1, tq, LANES), lambda bi, i: (bi, i, 0)),
                  pl.BlockSpec((1, s, LANES), lambda bi, i: (bi, 0, 0)),
                  pl.BlockSpec((1, s, LANES), lambda bi, i: (bi, 0, 0))],
        out_specs=pl.BlockSpec((1, nk, tq, sc), lambda bi, i: (bi, 0, i, 0)),
        out_shape=jax.ShapeDtypeStruct((b, nk, s, sc), BF16),
        scratch_shapes=[pltpu.VMEM((nk, tq, sc), I32), pltpu.VMEM((N_IDX_HEADS, tq, LANES), F32),
                        pltpu.VMEM((N_IDX_HEADS // 2 * tq, LANES), BF16), pltpu.VMEM((tq, LANES), I32)],
        compiler_params=_params(("parallel", "arbitrary")),
        name="dsa_select",
    )(qi, kw, ka, kb)


def _attn_kernel(qe_ref, qo_ref, k_ref, v_ref, bias_ref, o_ref, m_scr, l_scr, acc_scr, *, tq, sc):
    iq = pl.program_id(1)
    c = pl.program_id(2)
    nk = pl.num_programs(2)
    last = ((iq + 1) * tq - 1) // sc
    nl = sc // LANES
    lt = lax.broadcasted_iota(I32, (tq, LANES), 1) < HEAD_DIM

    @pl.when(c == 0)
    def _():
        m_scr[...] = jnp.full(m_scr.shape, -jnp.inf, F32)
        l_scr[...] = jnp.zeros(l_scr.shape, F32)
        acc_scr[...] = jnp.zeros(acc_scr.shape, F32)

    @pl.when(c <= last)
    def _():
        bias = bias_ref[0, 0].astype(F32)
        bias2 = jnp.concatenate([bias, bias], axis=0)
        for j in range(N_HEADS // 2):
            sl = slice(j * LANES, (j + 1) * LANES)
            q2 = jnp.concatenate([qe_ref[0, :, sl], qo_ref[0, :, sl]], axis=0)
            s = lax.dot_general(q2, k_ref[0, :, sl], _NT, preferred_element_type=F32) + bias2
            m_prev = m_scr[j]
            m_new = jnp.maximum(m_prev, jnp.max(s, axis=1, keepdims=True))
            p = jnp.exp2(s - (jnp.concatenate([m_new] * nl, axis=1) if nl > 1 else m_new))
            alpha = jnp.exp2(m_prev - m_new)
            l_scr[j] = alpha * l_scr[j] + jnp.sum(p, axis=1, keepdims=True)
            m_scr[j] = m_new
            pv = jnp.dot(p.astype(BF16), v_ref[0, :, sl], preferred_element_type=F32)
            acc = acc_scr[:, sl]
            acc_scr[:, sl] = jnp.where(lt, alpha[:tq] * acc + pv[:tq], alpha[tq:] * acc + pv[tq:])

    @pl.when(c == nk - 1)
    def _():
        for j in range(N_HEADS // 2):
            sl = slice(j * LANES, (j + 1) * LANES)
            l = jnp.where(lt, l_scr[j, :tq], l_scr[j, tq:])
            o_ref[0, :, sl] = (acc_scr[:, sl] / l).astype(BF16)


def _attention(qe, qo, k, v, bias, tq, sc):
    b, s, w = qe.shape
    nq, nk = s // tq, s // sc
    last = lambda i: ((i + 1) * tq - 1) // sc
    qspec = pl.BlockSpec((1, tq, w), lambda bi, i, c: (bi, i, 0))
    kspec = pl.BlockSpec((1, sc, w), lambda bi, i, c: (bi, jnp.minimum(c, last(i)), 0))
    kern = functools.partial(_attn_kernel, tq=tq, sc=sc)
    return pl.pallas_call(
        kern,
        grid=(b, nq, nk),
        in_specs=[qspec, qspec, kspec, kspec,
                  pl.BlockSpec((1, 1, tq, sc), lambda bi, i, c: (bi, jnp.minimum(c, last(i)), i, 0))],
        out_specs=pl.BlockSpec((1, tq, w), lambda bi, i, c: (bi, i, 0)),
        out_shape=jax.ShapeDtypeStruct((b, s, w), BF16),
        scratch_shapes=[pltpu.VMEM((N_HEADS // 2, 2 * tq, LANES), F32),
                        pltpu.VMEM((N_HEADS // 2, 2 * tq, LANES), F32), pltpu.VMEM((tq, w), F32)],
        compiler_params=_params(("parallel", "parallel", "arbitrary")),
        name="dsa_attention",
    )(qe, qo, k, v, bias)


def _outproj_kernel(xp_ref, halo_ref, attn_ref, x_ref, g1_ref, wpool_ref, ps_ref, wout_ref, o_ref,
                    ext_scr, *, tm, seq):
    i = pl.program_id(0)
    t0 = (i % (seq // tm)) * tm
    ext_scr[0:POOL_HALO, :] = halo_ref[...] * jnp.where(t0 == 0, 0.0, 1.0)
    ext_scr[POOL_HALO:POOL_HALO + tm, :] = xp_ref[...]
    t = t0 + lax.broadcasted_iota(I32, (tm, POOL_GROUP), 0)
    total = jnp.dot(attn_ref[...], wout_ref[D_POOL:D_POOL + D_ATTN, :], preferred_element_type=F32)
    for g, w in enumerate(POOL_WINDOWS):
        sl = slice(g * POOL_GROUP, (g + 1) * POOL_GROUP)
        cur = ext_scr[POOL_HALO:POOL_HALO + tm, sl]
        acc = cur
        for j in range(1, w):
            acc = acc + ext_scr[POOL_HALO - j:POOL_HALO - j + tm, sl]
        cnt = jnp.minimum(t + 1, w).astype(F32)
        pooled = acc / cnt - cur
        mixed = jnp.dot(pooled.astype(BF16), wpool_ref[g], preferred_element_type=F32) * ps_ref[:, sl]
        total = total + jnp.dot(mixed.astype(BF16), wout_ref[sl, :], preferred_element_type=F32)
    o_ref[...] = x_ref[...] + g1_ref[0] * total


def _out_projection(xp, attn, x2, g1, w_pool, pscale, w_out, seq, tm):
    n, d = x2.shape
    tps = seq // tm
    hb = tm // POOL_HALO
    row = lambda w: pl.BlockSpec((tm, w), lambda i: (i, 0))
    kern = functools.partial(_outproj_kernel, tm=tm, seq=seq)
    return pl.pallas_call(
        kern,
        grid=(n // tm,),
        in_specs=[row(D_POOL),
                  pl.BlockSpec((POOL_HALO, D_POOL), lambda i: (jnp.maximum(i * hb - 1, 0), 0)),
                  row(D_ATTN), row(d),
                  pl.BlockSpec((1, 1, d), lambda i: (i // tps, 0, 0)),
                  pl.BlockSpec(w_pool.shape, lambda i: (0, 0, 0)),
                  pl.BlockSpec((1, D_POOL), lambda i: (0, 0)),
                  pl.BlockSpec(w_out.shape, lambda i: (0, 0))],
        out_specs=row(d),
        out_shape=jax.ShapeDtypeStruct((n, d), F32),
        scratch_shapes=[pltpu.VMEM((tm + POOL_HALO, D_POOL), F32)],
        compiler_params=_params(("parallel",)),
        name="pool_out_proj",
    )(xp, xp, attn, x2, g1, w_pool, pscale, w_out)


EXPERT_SUB = 512
WDT = BF16
NO_RANK = 127.0


def _top_values(w, n, want_rank=False):
    outs = []
    rank = jnp.full(w.shape, NO_RANK, F32) if want_rank else None
    for r in range(n):
        mx = jnp.max(w, axis=0, keepdims=True)
        outs.append(mx)
        hit = w == mx
        if want_rank:
            rank = jnp.where(hit, float(r), rank)
        w = jnp.where(hit, -jnp.inf, w)
    return outs, rank


def _router_kernel(x_ref, sc_ref, sh_ref, g_ref, wq_ref, sk_ref,
                   h_ref, nrow_ref, scl_ref, rb_ref, e2_ref):
    hb = _mod_norm(x_ref[...], g_ref[...], sc_ref[0], sh_ref[0]).astype(BF16)
    h_ref[...] = hb
    tt = hb.shape[0]
    for hd in range(PEER_HEADS):
        s, tops, ranks = [], [], []
        for p in range(2):
            r0 = (hd * 2 + p) * N_KEYS
            qt = lax.dot_general(wq_ref[r0:r0 + N_KEYS, :], hb, _NT, preferred_element_type=F32)
            st = jnp.dot(sk_ref[hd, p], qt.astype(BF16), preferred_element_type=F32)
            s.append(st)
            top, rank = _top_values(st, PEER_TOPK, want_rank=True)
            tops.append(top)
            ranks.append(rank)
        row = lax.broadcasted_iota(I32, (PEER_TOPK, tt), 0)
        b_mat = jnp.zeros((PEER_TOPK, tt), F32)
        for r, brow in enumerate(tops[1]):
            b_mat = jnp.where(row == r, brow, b_mat)
        a_mat = jnp.zeros((PEER_TOPK, tt), F32)
        for r, arow in enumerate(tops[0]):
            a_mat = jnp.where(row == r, arow, a_mat)
        cands = [a + b_mat for a in tops[0]]
        neg = -jnp.inf
        row8 = lax.broadcasted_iota(I32, (8, tt), 0)
        stair =[cands[0], cands[1][:8],
                 jnp.where(row8 < 5, cands[2][:8], neg), jnp.where(row8 < 4, cands[3][:8], neg),
                 jnp.where(row >= 4, a_mat + tops[1][0], neg),
                 jnp.where(jnp.logical_and(row >= 4, row < 8), a_mat + tops[1][1], neg),
                 jnp.where(row == 4, a_mat + tops[1][2], neg)]
        best, _ = _top_values(jnp.concatenate(stair, axis=0), PEER_TOPK)
        theta = best[PEER_TOPK - 1]
        z = jnp.zeros_like(theta)
        for cval in best:
            z = z + jnp.exp(cval - best[0])
        nrow = jnp.zeros((N_KEYS, tt), F32)
        for r, cnd in enumerate(cands):
            n_r = jnp.sum(jnp.where(cnd >= theta, 1.0, 0.0), axis=0, keepdims=True)
            nrow = jnp.where(ranks[0] == float(r), n_r, nrow)
        nrow_ref[hd] = nrow
        scl_ref[hd] = jnp.exp(s[0] - tops[0][0]) / z
        rb_ref[hd] = ranks[1].astype(WDT)
        e2_ref[hd] = jnp.exp(s[1] - tops[1][0]).astype(WDT)


def _router(x2, sc, sh, g, wq_t, sk, seq, tt):
    n, d = x2.shape
    tps = seq // tt
    per_b = pl.BlockSpec((1, 1, d), lambda i: (i // tps, 0, 0))
    big = pl.BlockSpec((PEER_HEADS, N_KEYS, tt), lambda i: (0, 0, i))
    big_shape = lambda dt: jax.ShapeDtypeStruct((PEER_HEADS, N_KEYS, n), dt)
    return pl.pallas_call(
        _router_kernel,
        grid=(n // tt,),
        in_specs=[pl.BlockSpec((tt, d), lambda i: (i, 0)), per_b, per_b,
                  pl.BlockSpec((1, d), lambda i: (0, 0)),
                  pl.BlockSpec(wq_t.shape, lambda i: (0, 0)),
                  pl.BlockSpec(sk.shape, lambda i: (0, 0, 0, 0))],
        out_specs=[pl.BlockSpec((tt, d), lambda i: (i, 0)), big, big, big, big],
        out_shape=[jax.ShapeDtypeStruct((n, d), BF16), big_shape(F32), big_shape(F32),
                   big_shape(WDT), big_shape(WDT)],
        compiler_params=_params(("parallel",)),
        name="peer_router",
    )(x2, sc, sh, g, wq_t, sk)


def _expert_kernel(h_ref, u_ref, vt_ref, nrow_ref, scl_ref, rb_ref, e2_ref, x_ref, g2_ref,
                   o_ref, acc_scr, ct_scr, *, et):
    j = pl.program_id(1)
    nj = pl.num_programs(1)
    tt = h_ref.shape[0]

    @pl.when(j == 0)
    def _():
        acc_scr[...] = jnp.zeros(acc_scr.shape, F32)

    def rows(ref, hd, i1):
        r = jnp.broadcast_to(ref[hd, pl.ds(i1, 1), :], (16, tt)).astype(WDT)
        return jnp.concatenate([r] * (N_KEYS // 16), axis=0)

    for s0 in range(0, et, EXPERT_SUB):
        at = lax.dot_general(u_ref[s0:s0 + EXPERT_SUB, :], h_ref[...], _NT,
                             preferred_element_type=F32)
        for k in range(EXPERT_SUB // N_KEYS):
            i1 = (j * et + s0) // N_KEYS + k
            a = at[k * N_KEYS:(k + 1) * N_KEYS, :]
            w = jnp.zeros((N_KEYS, tt), WDT)
            for hd in range(PEER_HEADS):
                sel = rb_ref[hd] < rows(nrow_ref, hd, i1)
                w = w + jnp.where(sel, e2_ref[hd], jnp.zeros((), WDT)) * rows(scl_ref, hd, i1)
            gelu = 0.5 * a * (1.0 + lax.erf(a * (2.0 ** -0.5)))
            ct_scr[s0 + k * N_KEYS:s0 + (k + 1) * N_KEYS, :] = (w * gelu.astype(WDT)).astype(BF16)
    acc_scr[...] += jnp.dot(vt_ref[...], ct_scr[...], preferred_element_type=F32)

    @pl.when(j == nj - 1)
    def _():
        o_ref[...] = x_ref[...] + g2_ref[0] * acc_scr[...].T


def _experts(hb, u, vt, nrow, scl, rb, e2, x2, g2, seq, tt, et):
    n, d = x2.shape
    ne = u.shape[0]
    tps = seq // tt
    big = pl.BlockSpec((PEER_HEADS, N_KEYS, tt), lambda i, j: (0, 0, i))
    kern = functools.partial(_expert_kernel, et=et)
    return pl.pallas_call(
        kern,
        grid=(n // tt, ne // et),
        in_specs=[pl.BlockSpec((tt, d), lambda i, j: (i, 0)),
                  pl.BlockSpec((et, d), lambda i, j: (j, 0)),
                  pl.BlockSpec((d, et), lambda i, j: (0, j)),
                  big, big, big, big,
                  pl.BlockSpec((tt, d), lambda i, j: (i, 0)),
                  pl.BlockSpec((1, 1, d), lambda i, j: (i // tps, 0, 0))],
        out_specs=pl.BlockSpec((tt, d), lambda i, j: (i, 0)),
        out_shape=jax.ShapeDtypeStruct((n, d), F32),
        scratch_shapes=[pltpu.VMEM((d, tt), F32), pltpu.VMEM((et, tt), BF16)],
        compiler_params=_params(("parallel", "arbitrary")),
        name="peer_experts",
    )(hb, u, vt, nrow, scl, rb, e2, x2, g2)


def _norm_kernel(x_ref, g_ref, o_ref):
    x = x_ref[...]
    o_ref[...] = x * lax.rsqrt(jnp.mean(x * x, axis=-1, keepdims=True) + EPS) * g_ref[...]


def _final_norm(x2, g, tm):
    n, d = x2.shape
    return pl.pallas_call(
        _norm_kernel,
        grid=(n // tm,),
        in_specs=[pl.BlockSpec((tm, d), lambda i: (i, 0)), pl.BlockSpec((1, d), lambda i: (0, 0))],
        out_specs=pl.BlockSpec((tm, d), lambda i: (i, 0)),
        out_shape=jax.ShapeDtypeStruct((n, d), F32),
        compiler_params=_params(("parallel",)),
        name="final_norm",
    )(x2, g)


def _rope_tables(positions):
    inv = ROPE_THETA ** (-jnp.arange(0, 2 * ROT_HALF, 2, dtype=F32) / (2 * ROT_HALF))
    ang = positions.astype(F32).reshape(-1, 1) * inv
    cos, sin = jnp.cos(ang), jnp.sin(ang)
    d = jnp.arange(LANES) % HEAD_DIM
    cosl, sinl = cos[:, d % ROT_HALF], sin[:, d % ROT_HALF]
    c = jnp.where(d < 2 * ROT_HALF, cosl, 1.0)
    s1 = jnp.where((d >= ROT_HALF) & (d < 2 * ROT_HALF), sinl, 0.0)
    s2 = jnp.where(d < ROT_HALF, -sinl, 0.0)
    return c, s1, s2


def _tile(dim, want):
    return min(dim, want)


def kernel(x, c, positions, w_ada, b_ada, norm_mix, w_in, w_pool, pool_scale, w_out, norm_ffn,
           w_query, sub_keys, expert_u, expert_v, final_norm):
    b, s, d = x.shape
    depth = w_ada.shape[0]
    n = b * s
    topk = min(TOPK_MAX, s // 4)
    tm = _tile(s, 512)
    tq = _tile(s, 512)
    sc = _tile(s, 512)
    tt = _tile(s, 512)
    et = 4 * EXPERT_SUB

    rc, rs1, rs2 = _rope_tables(positions)
    mod = _modulation(c, w_ada, b_ada)
    d_in = w_in.shape[2]
    pad = (-d_in) % LANES
    x2 = x.reshape(n, d)
    for l in range(depth):
        sh1, sc1, g1, sh2, sc2, g2 = [mod[l, :, i * d:(i + 1) * d].reshape(b, 1, d) for i in range(6)]
        w_pad = jnp.pad(w_in[l], ((0, 0), (0, pad))).astype(BF16)
        xp, qe, qo, k, v, qi, kw, ka, kb = _in_projection(
            x2, sc1, sh1, norm_mix[l].reshape(1, d), w_pad, rc, rs1, rs2, s, tm)
        r3 = lambda a: a.reshape(b, s, a.shape[-1])
        bias = _select(r3(qi), r3(kw), r3(ka), r3(kb), topk, tq, sc)
        attn = _attention(r3(qe), r3(qo), r3(k), r3(v), bias, tq, sc).reshape(n, D_ATTN)
        x2 = _out_projection(xp, attn, x2, g1, w_pool[l].astype(BF16), pool_scale[l].reshape(1, D_POOL),
                             w_out[l].astype(BF16), s, tm)
        hb, nrow, scl, rb, e2 = _router(x2, sc2, sh2, norm_ffn[l].reshape(1, d),
                                        w_query[l].T.astype(BF16), sub_keys[l].astype(BF16), s, tt)
        x2 = _experts(hb, expert_u[l].astype(BF16), expert_v[l].T.astype(BF16), nrow, scl, rb, e2,
                      x2, g2, s, tt, et)
    return _final_norm(x2, final_norm.reshape(1, d), tm).reshape(b, s, d)
```

```python
import functools

import jax
import jax.numpy as jnp
from jax import lax
from jax.experimental import pallas as pl
from jax.experimental.pallas import tpu as pltpu

F32 = jnp.float32
BF16 = jnp.bfloat16
I32 = jnp.int32

EPS = 1e-6
ROPE_THETA = 500000.0
N_HEADS = 8
HEAD_DIM = 64
ROT_HALF = 8
D_POOL = 512
D_ATTN = 512
POOL_WINDOWS = (2, 4, 8, 16)
POOL_GROUP = 128
POOL_HALO = 16
N_IDX_HEADS = 8
IDX_DIM = 64
TOPK_MAX = 256
N_KEYS = 128
PEER_HEADS = 8
PEER_TOPK = 16
LANES = 128
INT_MIN = -2147483648
LOG2E = 1.4426950408889634
NEG_BIG = -1e30
VMEM_LIMIT = 56 * 1024 * 1024

_NT = (((1,), (1,)), ((), ()))


def _params(sem):
    return pltpu.CompilerParams(dimension_semantics=sem, vmem_limit_bytes=VMEM_LIMIT)


def _mod_kernel(c_ref, w_ref, b_ref, o_ref):
    c = c_ref[...]
    s = c / (1.0 + jnp.exp(-c))
    o_ref[0] = jnp.dot(s, w_ref[0], preferred_element_type=F32,
                       precision=lax.Precision.HIGHEST) + b_ref[0]


def _modulation(c, w_ada, b_ada):
    depth, d, d6 = w_ada.shape
    b = c.shape[0]
    nj = d6 // d
    return pl.pallas_call(
        _mod_kernel,
        grid=(depth, nj),
        in_specs=[pl.BlockSpec((b, d), lambda l, j: (0, 0)),
                  pl.BlockSpec((1, d, d), lambda l, j: (l, 0, j)),
                  pl.BlockSpec((1, 1, d), lambda l, j: (l, 0, j))],
        out_specs=pl.BlockSpec((1, b, d), lambda l, j: (l, 0, j)),
        out_shape=jax.ShapeDtypeStruct((depth, b, d6), F32),
        compiler_params=_params(("parallel", "parallel")),
        name="adaln_mod",
    )(c, w_ada, b_ada.reshape(depth, 1, d6))


def _mod_norm(x, g, sc, sh):
    y = x * lax.rsqrt(jnp.mean(x * x, axis=-1, keepdims=True) + EPS)
    return (y * g) * (1.0 + sc) + sh


def _rope(z, c, s1, s2):
    n = z.shape[1] // LANES
    if n > 1:
        c = jnp.concatenate([c] * n, axis=1)
        s1 = jnp.concatenate([s1] * n, axis=1)
        s2 = jnp.concatenate([s2] * n, axis=1)
    w = z.shape[1]
    return z * c + pltpu.roll(z, ROT_HALF, 1) * s1 + pltpu.roll(z, w - ROT_HALF, 1) * s2


def _inproj_kernel(x_ref, sc_ref, sh_ref, g_ref, w_ref, c_ref, s1_ref, s2_ref,
                   xp_ref, qe_ref, qo_ref, k_ref, v_ref, qi_ref, kw_ref, ka_ref, kb_ref):
    h = _mod_norm(x_ref[...], g_ref[...], sc_ref[0], sh_ref[0]).astype(BF16)
    c, s1, s2 = c_ref[...], s1_ref[...], s2_ref[...]

    def proj(lo, hi):
        return jnp.dot(h, w_ref[:, lo:hi], preferred_element_type=F32)

    xp_ref[...] = proj(0, 512)
    lane = lax.broadcasted_iota(I32, (h.shape[0], D_ATTN), 1)
    even = (lane % LANES) < HEAD_DIM
    q = _rope(proj(512, 1024), c, s1, s2) * (HEAD_DIM ** -0.5 * LOG2E)
    qe_ref[...] = jnp.where(even, q, 0.0).astype(BF16)
    qo_ref[...] = jnp.where(even, 0.0, q).astype(BF16)
    k_ref[...] = _rope(proj(1024, 1536), c, s1, s2).astype(BF16)
    v_ref[...] = proj(1536, 2048).astype(BF16)
    qi_ref[...] = (_rope(proj(2048, 2560), c, s1, s2) * (IDX_DIM ** -0.5)).astype(BF16)
    lane1 = lax.broadcasted_iota(I32, (h.shape[0], LANES), 1)
    is_k = lane1 < IDX_DIM
    kw = _rope(proj(2560, 2688), jnp.where(is_k, c, 1.0), jnp.where(is_k, s1, 0.0),
               jnp.where(is_k, s2, 0.0))
    kw = kw * jnp.where(is_k, 1.0, N_IDX_HEADS ** -0.5)
    kw_ref[...] = kw
    ka = jnp.where(is_k, kw, 0.0)
    ka_ref[...] = ka.astype(BF16)
    kb_ref[...] = pltpu.roll(ka, IDX_DIM, 1).astype(BF16)


def _in_projection(x2, sc, sh, g, w_pad, rc, rs1, rs2, seq, tm):
    n, d = x2.shape
    tps = seq // tm
    row = lambda w: pl.BlockSpec((tm, w), lambda i: (i, 0))
    per_b = pl.BlockSpec((1, 1, d), lambda i: (i // tps, 0, 0))
    out_w = [(512, F32)] + [(512, BF16)] * 5 + [(LANES, F32), (LANES, BF16), (LANES, BF16)]
    return pl.pallas_call(
        _inproj_kernel,
        grid=(n // tm,),
        in_specs=[row(d), per_b, per_b, pl.BlockSpec((1, d), lambda i: (0, 0)),
                  pl.BlockSpec(w_pad.shape, lambda i: (0, 0)),
                  row(LANES), row(LANES), row(LANES)],
        out_specs=[row(w) for w, _ in out_w],
        out_shape=[jax.ShapeDtypeStruct((n, w), dt) for w, dt in out_w],
        compiler_params=_params(("parallel",)),
        name="in_proj",
    )(x2, sc, sh, g, w_pad, rc, rs1, rs2)


def _select_kernel(qi_ref, kw_ref, ka_ref, kb_ref, bias_ref, keys_scr, wb_scr, qs_scr, cut_scr, *,
                   tq, sc, topk):
    iq = pl.program_id(1)
    n_total = bias_ref.shape[1]
    nck = ((iq + 1) * tq + sc - 1) // sc
    kw = kw_ref[0]
    for h in range(N_IDX_HEADS):
        wb_scr[h] = jnp.broadcast_to(kw[:, IDX_DIM + h:IDX_DIM + h + 1], (tq, LANES))
    for j in range(N_IDX_HEADS // 2):
        qs_scr[j * tq:(j + 1) * tq, :] = qi_ref[0, :, j * LANES:(j + 1) * LANES]
    nl = sc // LANES
    row = iq * tq + lax.broadcasted_iota(I32, (tq, sc), 0)
    col0 = lax.broadcasted_iota(I32, (tq, sc), 1)

    def tile(a):
        return jnp.concatenate([a] * nl, axis=1) if nl > 1 else a

    def score_body(c, carry):
        off = pl.multiple_of(c * sc, sc)
        ka = ka_ref[0, pl.ds(off, sc), :]
        kb = kb_ref[0, pl.ds(off, sc), :]
        d_even = lax.dot_general(qs_scr[...], ka, _NT, preferred_element_type=F32)
        d_odd = lax.dot_general(qs_scr[...], kb, _NT, preferred_element_type=F32)
        score = jnp.zeros((tq, sc), F32)
        for j in range(N_IDX_HEADS // 2):
            rs = slice(j * tq, (j + 1) * tq)
            score = score + jnp.maximum(d_even[rs], 0.0) * tile(wb_scr[2 * j])
            score = score + jnp.maximum(d_odd[rs], 0.0) * tile(wb_scr[2 * j + 1])
        bits = pltpu.bitcast(score, I32)
        key = bits ^ ((bits >> 31) & 0x7FFFFFFF)
        key = jnp.where(score == 0.0, 0, key)
        keys_scr[c] = jnp.where(col0 + off <= row, key, INT_MIN)
        return carry

    lax.fori_loop(0, nck, score_body, 0)

    rb = min(tq, 128)
    rb_tie = min(tq, 32)
    lane = lax.broadcasted_iota(I32, (rb_tie, LANES), 1)

    def count(hit, *row_args, rows=rb, gates=None):
        parts = []
        for b, r0 in enumerate(range(0, tq, rows)):
            blk = [a[r0:r0 + rows] for a in row_args]

            def cnt_body(c, acc, r0=r0, blk=blk):
                k = keys_scr[c, r0:r0 + rows, :]
                for j in range(nl):
                    acc = acc + hit(k[:, j * LANES:(j + 1) * LANES], c * sc + j * LANES, *blk)
                return acc

            zero = jnp.zeros((rows, LANES), F32)
            scan = functools.partial(lax.fori_loop, 0, nck, cnt_body, zero)
            parts.append(scan() if gates is None else lax.cond(gates[b], scan, lambda zero=zero: zero))
        part = jnp.concatenate(parts, axis=0) if len(parts) > 1 else parts[0]
        return jnp.sum(part, axis=1, keepdims=True)

    def bit_body(i, carry):
        th, n_ge = carry
        cand = th + (jnp.int32(1) << (31 - i))
        cnt = count(lambda k, off, cb: jnp.where(k >= cb, 1.0, 0.0), cand)
        ok = cnt >= float(topk)
        return jnp.where(ok, cand, th), jnp.where(ok, cnt, n_ge)

    th, n_ge = lax.fori_loop(0, 32, bit_body, (jnp.full((tq, LANES), INT_MIN, I32),
                                               jnp.zeros((tq, LANES), F32)))
    excess = jnp.logical_and(n_ge > float(topk), th != INT_MIN)
    cut_scr[...] = jnp.full((tq, LANES), n_total * sc, I32)

    excess_f = jnp.where(excess, 1.0, 0.0)

    @pl.when(jnp.max(excess_f) > 0.0)
    def _():
        gates = [jnp.max(excess_f[r0:r0 + rb_tie]) > 0.0 for r0 in range(0, tq, rb_tie)]
        n_gt = count(lambda k, off, tb: jnp.where(k > tb, 1.0, 0.0), th, rows=rb_tie, gates=gates)
        need = float(topk) - n_gt

        def cut_body(i, cut):
            cand = cut + (jnp.int32(n_total * sc // 2) >> i)
            before = count(lambda k, off, tb, cb: jnp.where(k == tb, jnp.where(lane + off < cb, 1.0, 0.0), 0.0),
                           th, cand, rows=rb_tie, gates=gates)
            return jnp.where(before < need, cand, cut)

        n_bits = (n_total * sc).bit_length() - 1
        cut = lax.fori_loop(0, n_bits, cut_body, jnp.zeros((tq, LANES), I32))
        cut_scr[...] = jnp.where(excess, cut, n_total * sc)

    th = tile(jnp.maximum(th, INT_MIN + 1))
    cut = tile(cut_scr[...])

    def out_body(c, carry):
        k = keys_scr[c]
        tie = jnp.where(k == th, jnp.where(col0 + c * sc <= cut, 0.0, NEG_BIG), NEG_BIG)
        bias_ref[0, c] = jnp.where(k > th, 0.0, tie).astype(BF16)
        return carry

    lax.fori_loop(0, nck, out_body, 0)

    def fill_body(c, carry):
        bias_ref[0, c] = jnp.full((tq, sc), NEG_BIG, BF16)
        return carry

    lax.fori_loop(nck, n_total, fill_body, 0)


def _select(qi, kw, ka, kb, topk, tq, sc):
    b, s, _ = qi.shape
    nq, nk = s // tq, s // sc
    kern = functools.partial(_select_kernel, tq=tq, sc=sc, topk=topk)
    return pl.pallas_call(
        kern,
        grid=(b, nq),
        in_specs=[pl.BlockSpec((1, tq, 512), lambda bi, i: (bi, i, 0)),
                  pl.BlockSpec((1, tq, LANES), lambda bi, i: (bi, i, 0)),
                  pl.BlockSpec((1, s, LANES), lambda bi, i: (bi, 0, 0)),
                  pl.BlockSpec((1, s, LANES), lambda bi, i: (bi, 0, 0))],
        out_specs=pl.BlockSpec((1, nk, tq, sc), lambda bi, i: (bi, 0, i, 0)),
        out_shape=jax.ShapeDtypeStruct((b, nk, s, sc), BF16),
        scratch_shapes=[pltpu.VMEM((nk, tq, sc), I32), pltpu.VMEM((N_IDX_HEADS, tq, LANES), F32),
                        pltpu.VMEM((N_IDX_HEADS // 2 * tq, LANES), BF16), pltpu.VMEM((tq, LANES), I32)],
        compiler_params=_params(("parallel", "arbitrary")),
        name="dsa_select",
    )(qi, kw, ka, kb)


def _attn_kernel(qe_ref, qo_ref, k_ref, v_ref, bias_ref, o_ref, m_scr, l_scr, acc_scr, *, tq, sc):
    iq = pl.program_id(1)
    c = pl.program_id(2)
    nk = pl.num_programs(2)
    last = ((iq + 1) * tq - 1) // sc
    nl = sc // LANES
    lt = lax.broadcasted_iota(I32, (tq, LANES), 1) < HEAD_DIM

    @pl.when(c == 0)
    def _():
        m_scr[...] = jnp.full(m_scr.shape, -jnp.inf, F32)
        l_scr[...] = jnp.zeros(l_scr.shape, F32)
        acc_scr[...] = jnp.zeros(acc_scr.shape, F32)

    @pl.when(c <= last)
    def _():
        bias = bias_ref[0, 0].astype(F32)
        bias2 = jnp.concatenate([bias, bias], axis=0)
        for j in range(N_HEADS // 2):
            sl = slice(j * LANES, (j + 1) * LANES)
            q2 = jnp.concatenate([qe_ref[0, :, sl], qo_ref[0, :, sl]], axis=0)
            s = lax.dot_general(q2, k_ref[0, :, sl], _NT, preferred_element_type=F32) + bias2
            m_prev = m_scr[j]
            m_new = jnp.maximum(m_prev, jnp.max(s, axis=1, keepdims=True))
            p = jnp.exp2(s - (jnp.concatenate([m_new] * nl, axis=1) if nl > 1 else m_new))
            alpha = jnp.exp2(m_prev - m_new)
            l_scr[j] = alpha * l_scr[j] + jnp.sum(p, axis=1, keepdims=True)
            m_scr[j] = m_new
            pv = jnp.dot(p.astype(BF16), v_ref[0, :, sl], preferred_element_type=F32)
            acc = acc_scr[:, sl]
            acc_scr[:, sl] = jnp.where(lt, alpha[:tq] * acc + pv[:tq], alpha[tq:] * acc + pv[tq:])

    @pl.when(c == nk - 1)
    def _():
        for j in range(N_HEADS // 2):
            sl = slice(j * LANES, (j + 1) * LANES)
            l = jnp.where(lt, l_scr[j, :tq], l_scr[j, tq:])
            o_ref[0, :, sl] = (acc_scr[:, sl] / l).astype(BF16)


def _attention(qe, qo, k, v, bias, tq, sc):
    b, s, w = qe.shape
    nq, nk = s // tq, s // sc
    last = lambda i: ((i + 1) * tq - 1) // sc
    qspec = pl.BlockSpec((1, tq, w), lambda bi, i, c: (bi, i, 0))
    kspec = pl.BlockSpec((1, sc, w), lambda bi, i, c: (bi, jnp.minimum(c, last(i)), 0))
    kern = functools.partial(_attn_kernel, tq=tq, sc=sc)
    return pl.pallas_call(
        kern,
        grid=(b, nq, nk),
        in_specs=[qspec, qspec, kspec, kspec,
                  pl.BlockSpec((1, 1, tq, sc), lambda bi, i, c: (bi, jnp.minimum(c, last(i)), i, 0))],
        out_specs=pl.BlockSpec((1, tq, w), lambda bi, i, c: (bi, i, 0)),
        out_shape=jax.ShapeDtypeStruct((b, s, w), BF16),
        scratch_shapes=[pltpu.VMEM((N_HEADS // 2, 2 * tq, LANES), F32),
                        pltpu.VMEM((N_HEADS // 2, 2 * tq, LANES), F32), pltpu.VMEM((tq, w), F32)],
        compiler_params=_params(("parallel", "parallel", "arbitrary")),
        name="dsa_attention",
    )(qe, qo, k, v, bias)


def _outproj_kernel(xp_ref, halo_ref, attn_ref, x_ref, g1_ref, wpool_ref, ps_ref, wout_ref, o_ref,
                    ext_scr, *, tm, seq):
    i = pl.program_id(0)
    t0 = (i % (seq // tm)) * tm
    ext_scr[0:POOL_HALO, :] = halo_ref[...] * jnp.where(t0 == 0, 0.0, 1.0)
    ext_scr[POOL_HALO:POOL_HALO + tm, :] = xp_ref[...]
    t = t0 + lax.broadcasted_iota(I32, (tm, POOL_GROUP), 0)
    total = jnp.dot(attn_ref[...], wout_ref[D_POOL:D_POOL + D_ATTN, :], preferred_element_type=F32)
    for g, w in enumerate(POOL_WINDOWS):
        sl = slice(g * POOL_GROUP, (g + 1) * POOL_GROUP)
        cur = ext_scr[POOL_HALO:POOL_HALO + tm, sl]
        acc = cur
        for j in range(1, w):
            acc = acc + ext_scr[POOL_HALO - j:POOL_HALO - j + tm, sl]
        cnt = jnp.minimum(t + 1, w).astype(F32)
        pooled = acc / cnt - cur
        mixed = jnp.dot(pooled.astype(BF16), wpool_ref[g], preferred_element_type=F32) * ps_ref[:, sl]
        total = total + jnp.dot(mixed.astype(BF16), wout_ref[sl, :], preferred_element_type=F32)
    o_ref[...] = x_ref[...] + g1_ref[0] * total


def _out_projection(xp, attn, x2, g1, w_pool, pscale, w_out, seq, tm):
    n, d = x2.shape
    tps = seq // tm
    hb = tm // POOL_HALO
    row = lambda w: pl.BlockSpec((tm, w), lambda i: (i, 0))
    kern = functools.partial(_outproj_kernel, tm=tm, seq=seq)
    return pl.pallas_call(
        kern,
        grid=(n // tm,),
        in_specs=[row(D_POOL),
                  pl.BlockSpec((POOL_HALO, D_POOL), lambda i: (jnp.maximum(i * hb - 1, 0), 0)),
                  row(D_ATTN), row(d),
                  pl.BlockSpec((1, 1, d), lambda i: (i // tps, 0, 0)),
                  pl.BlockSpec(w_pool.shape, lambda i: (0, 0, 0)),
                  pl.BlockSpec((1, D_POOL), lambda i: (0, 0)),
                  pl.BlockSpec(w_out.shape, lambda i: (0, 0))],
        out_specs=row(d),
        out_shape=jax.ShapeDtypeStruct((n, d), F32),
        scratch_shapes=[pltpu.VMEM((tm + POOL_HALO, D_POOL), F32)],
        compiler_params=_params(("parallel",)),
        name="pool_out_proj",
    )(xp, xp, attn, x2, g1, w_pool, pscale, w_out)


EXPERT_SUB = 512
WDT = BF16
NO_RANK = 127.0


def _top_values(w, n, want_rank=False):
    outs = []
    rank = jnp.full(w.shape, NO_RANK, F32) if want_rank else None
    for r in range(n):
        mx = jnp.max(w, axis=0, keepdims=True)
        outs.append(mx)
        hit = w == mx
        if want_rank:
            rank = jnp.where(hit, float(r), rank)
        w = jnp.where(hit, -jnp.inf, w)
    return outs, rank


def _router_kernel(x_ref, sc_ref, sh_ref, g_ref, wq_ref, sk_ref,
                   h_ref, nrow_ref, scl_ref, rb_ref, e2_ref):
    hb = _mod_norm(x_ref[...], g_ref[...], sc_ref[0], sh_ref[0]).astype(BF16)
    h_ref[...] = hb
    tt = hb.shape[0]
    for hd in range(PEER_HEADS):
        s, tops, ranks = [], [], []
        for p in range(2):
            r0 = (hd * 2 + p) * N_KEYS
            qt = lax.dot_general(wq_ref[r0:r0 + N_KEYS, :], hb, _NT, preferred_element_type=F32)
            st = jnp.dot(sk_ref[hd, p], qt.astype(BF16), preferred_element_type=F32)
            s.append(st)
            top, rank = _top_values(st, PEER_TOPK, want_rank=True)
            tops.append(top)
            ranks.append(rank)
        row = lax.broadcasted_iota(I32, (PEER_TOPK, tt), 0)
        b_mat = jnp.zeros((PEER_TOPK, tt), F32)
        for r, brow in enumerate(tops[1]):
            b_mat = jnp.where(row == r, brow, b_mat)
        a_mat = jnp.zeros((PEER_TOPK, tt), F32)
        for r, arow in enumerate(tops[0]):
            a_mat = jnp.where(row == r, arow, a_mat)
        cands = [a + b_mat for a in tops[0]]
        neg = -jnp.inf
        row8 = lax.broadcasted_iota(I32, (8, tt), 0)
        stair =[cands[0], cands[1][:8],
                 jnp.where(row8 < 5, cands[2][:8], neg), jnp.where(row8 < 4, cands[3][:8], neg),
                 jnp.where(row >= 4, a_mat + tops[1][0], neg),
                 jnp.where(jnp.logical_and(row >= 4, row < 8), a_mat + tops[1][1], neg),
                 jnp.where(row == 4, a_mat + tops[1][2], neg)]
        best, _ = _top_values(jnp.concatenate(stair, axis=0), PEER_TOPK)
        theta = best[PEER_TOPK - 1]
        z = jnp.zeros_like(theta)
        for cval in best:
            z = z + jnp.exp(cval - best[0])
        nrow = jnp.zeros((N_KEYS, tt), F32)
        for r, cnd in enumerate(cands):
            n_r = jnp.sum(jnp.where(cnd >= theta, 1.0, 0.0), axis=0, keepdims=True)
            nrow = jnp.where(ranks[0] == float(r), n_r, nrow)
        nrow_ref[hd] = nrow
        scl_ref[hd] = jnp.exp(s[0] - tops[0][0]) / z
        rb_ref[hd] = ranks[1].astype(WDT)
        e2_ref[hd] = jnp.exp(s[1] - tops[1][0]).astype(WDT)


def _router(x2, sc, sh, g, wq_t, sk, seq, tt):
    n, d = x2.shape
    tps = seq // tt
    per_b = pl.BlockSpec((1, 1, d), lambda i: (i // tps, 0, 0))
    big = pl.BlockSpec((PEER_HEADS, N_KEYS, tt), lambda i: (0, 0, i))
    big_shape = lambda dt: jax.ShapeDtypeStruct((PEER_HEADS, N_KEYS, n), dt)
    return pl.pallas_call(
        _router_kernel,
        grid=(n // tt,),
        in_specs=[pl.BlockSpec((tt, d), lambda i: (i, 0)), per_b, per_b,
                  pl.BlockSpec((1, d), lambda i: (0, 0)),
                  pl.BlockSpec(wq_t.shape, lambda i: (0, 0)),
                  pl.BlockSpec(sk.shape, lambda i: (0, 0, 0, 0))],
        out_specs=[pl.BlockSpec((tt, d), lambda i: (i, 0)), big, big, big, big],
        out_shape=[jax.ShapeDtypeStruct((n, d), BF16), big_shape(F32), big_shape(F32),
                   big_shape(WDT), big_shape(WDT)],
        compiler_params=_params(("parallel",)),
        name="peer_router",
    )(x2, sc, sh, g, wq_t, sk)


def _expert_kernel(h_ref, u_ref, vt_ref, nrow_ref, scl_ref, rb_ref, e2_ref, x_ref, g2_ref,
                   o_ref, acc_scr, ct_scr, *, et):
    j = pl.program_id(1)
    nj = pl.num_programs(1)
    tt = h_ref.shape[0]

    @pl.when(j == 0)
    def _():
        acc_scr[...] = jnp.zeros(acc_scr.shape, F32)

    def rows(ref, hd, i1):
        r = jnp.broadcast_to(ref[hd, pl.ds(i1, 1), :], (16, tt)).astype(WDT)
        return jnp.concatenate([r] * (N_KEYS // 16), axis=0)

    for s0 in range(0, et, EXPERT_SUB):
        at = lax.dot_general(u_ref[s0:s0 + EXPERT_SUB, :], h_ref[...], _NT,
                             preferred_element_type=F32)
        for k in range(EXPERT_SUB // N_KEYS):
            i1 = (j * et + s0) // N_KEYS + k
            a = at[k * N_KEYS:(k + 1) * N_KEYS, :]
            w = jnp.zeros((N_KEYS, tt), WDT)
            for hd in range(PEER_HEADS):
                sel = rb_ref[hd] < rows(nrow_ref, hd, i1)
                w = w + jnp.where(sel, e2_ref[hd], jnp.zeros((), WDT)) * rows(scl_ref, hd, i1)
            gelu = 0.5 * a * (1.0 + lax.erf(a * (2.0 ** -0.5)))
            ct_scr[s0 + k * N_KEYS:s0 + (k + 1) * N_KEYS, :] = (w * gelu.astype(WDT)).astype(BF16)
    acc_scr[...] += jnp.dot(vt_ref[...], ct_scr[...], preferred_element_type=F32)

    @pl.when(j == nj - 1)
    def _():
        o_ref[...] = x_ref[...] + g2_ref[0] * acc_scr[...].T


def _experts(hb, u, vt, nrow, scl, rb, e2, x2, g2, seq, tt, et):
    n, d = x2.shape
    ne = u.shape[0]
    tps = seq // tt
    big = pl.BlockSpec((PEER_HEADS, N_KEYS, tt), lambda i, j: (0, 0, i))
    kern = functools.partial(_expert_kernel, et=et)
    return pl.pallas_call(
        kern,
        grid=(n // tt, ne // et),
        in_specs=[pl.BlockSpec((tt, d), lambda i, j: (i, 0)),
                  pl.BlockSpec((et, d), lambda i, j: (j, 0)),
                  pl.BlockSpec((d, et), lambda i, j: (0, j)),
                  big, big, big, big,
                  pl.BlockSpec((tt, d), lambda i, j: (i, 0)),
                  pl.BlockSpec((1, 1, d), lambda i, j: (i // tps, 0, 0))],
        out_specs=pl.BlockSpec((tt, d), lambda i, j: (i, 0)),
        out_shape=jax.ShapeDtypeStruct((n, d), F32),
        scratch_shapes=[pltpu.VMEM((d, tt), F32), pltpu.VMEM((et, tt), BF16)],
        compiler_params=_params(("parallel", "arbitrary")),
        name="peer_experts",
    )(hb, u, vt, nrow, scl, rb, e2, x2, g2)


def _norm_kernel(x_ref, g_ref, o_ref):
    x = x_ref[...]
    o_ref[...] = x * lax.rsqrt(jnp.mean(x * x, axis=-1, keepdims=True) + EPS) * g_ref[...]


def _final_norm(x2, g, tm):
    n, d = x2.shape
    return pl.pallas_call(
        _norm_kernel,
        grid=(n // tm,),
        in_specs=[pl.BlockSpec((tm, d), lambda i: (i, 0)), pl.BlockSpec((1, d), lambda i: (0, 0))],
        out_specs=pl.BlockSpec((tm, d), lambda i: (i, 0)),
        out_shape=jax.ShapeDtypeStruct((n, d), F32),
        compiler_params=_params(("parallel",)),
        name="final_norm",
    )(x2, g)


def _rope_tables(positions):
    inv = ROPE_THETA ** (-jnp.arange(0, 2 * ROT_HALF, 2, dtype=F32) / (2 * ROT_HALF))
    ang = positions.astype(F32).reshape(-1, 1) * inv
    cos, sin = jnp.cos(ang), jnp.sin(ang)
    d = jnp.arange(LANES) % HEAD_DIM
    cosl, sinl = cos[:, d % ROT_HALF], sin[:, d % ROT_HALF]
    c = jnp.where(d < 2 * ROT_HALF, cosl, 1.0)
    s1 = jnp.where((d >= ROT_HALF) & (d < 2 * ROT_HALF), sinl, 0.0)
    s2 = jnp.where(d < ROT_HALF, -sinl, 0.0)
    return c, s1, s2


def _tile(dim, want):
    return min(dim, want)


def kernel(x, c, positions, w_ada, b_ada, norm_mix, w_in, w_pool, pool_scale, w_out, norm_ffn,
           w_query, sub_keys, expert_u, expert_v, final_norm):
    b, s, d = x.shape
    depth = w_ada.shape[0]
    n = b * s
    topk = min(TOPK_MAX, s // 4)
    tm = _tile(s, 512)
    tq = _tile(s, 512)
    sc = _tile(s, 1024)
    tt = _tile(s, 512)
    et = 4 * EXPERT_SUB

    rc, rs1, rs2 = _rope_tables(positions)
    mod = _modulation(c, w_ada, b_ada)
    d_in = w_in.shape[2]
    pad = (-d_in) % LANES
    x2 = x.reshape(n, d)
    for l in range(depth):
        sh1, sc1, g1, sh2, sc2, g2 = [mod[l, :, i * d:(i + 1) * d].reshape(b, 1, d) for i in range(6)]
        w_pad = jnp.pad(w_in[l], ((0, 0), (0, pad))).astype(BF16)
        xp, qe, qo, k, v, qi, kw, ka, kb = _in_projection(
            x2, sc1, sh1, norm_mix[l].reshape(1, d), w_pad, rc, rs1, rs2, s, tm)
        r3 = lambda a: a.reshape(b, s, a.shape[-1])
        bias = _select(r3(qi), r3(kw), r3(ka), r3(kb), topk, tq, sc)
        attn = _attention(r3(qe), r3(qo), r3(k), r3(v), bias, tq, sc).reshape(n, D_ATTN)
        x2 = _out_projection(xp, attn, x2, g1, w_pool[l].astype(BF16), pool_scale[l].reshape(1, D_POOL),
                             w_out[l].astype(BF16), s, tm)
        hb, nrow, scl, rb, e2 = _router(x2, sc2, sh2, norm_ffn[l].reshape(1, d),
                                        w_query[l].T.astype(BF16), sub_keys[l].astype(BF16), s, tt)
        x2 = _experts(hb, expert_u[l].astype(BF16), expert_v[l].T.astype(BF16), nrow, scl, rb, e2,
                      x2, g2, s, tt, et)
    return _final_norm(x2, final_norm.reshape(1, d), tm).reshape(b, s, d)
```

```python
import functools

import jax
import jax.numpy as jnp
from jax import lax
from jax.experimental import pallas as pl
from jax.experimental.pallas import tpu as pltpu

F32 = jnp.float32
BF16 = jnp.bfloat16
I32 = jnp.int32

EPS = 1e-6
ROPE_THETA = 500000.0
N_HEADS = 8
HEAD_DIM = 64
ROT_HALF = 8
D_POOL = 512
D_ATTN = 512
POOL_WINDOWS = (2, 4, 8, 16)
POOL_GROUP = 128
POOL_HALO = 16
N_IDX_HEADS = 8
IDX_DIM = 64
TOPK_MAX = 256
N_KEYS = 128
PEER_HEADS = 8
PEER_TOPK = 16
LANES = 128
INT_MIN = -2147483648
LOG2E = 1.4426950408889634
NEG_BIG = -1e30
VMEM_LIMIT = 56 * 1024 * 1024

_NT = (((1,), (1,)), ((), ()))


def _params(sem):
    return pltpu.CompilerParams(dimension_semantics=sem, vmem_limit_bytes=VMEM_LIMIT)


def _mod_kernel(c_ref, w_ref, b_ref, o_ref):
    c = c_ref[...]
    s = c / (1.0 + jnp.exp(-c))
    o_ref[0] = jnp.dot(s, w_ref[0], preferred_element_type=F32,
                       precision=lax.Precision.HIGHEST) + b_ref[0]


def _modulation(c, w_ada, b_ada):
    depth, d, d6 = w_ada.shape
    b = c.shape[0]
    nj = d6 // d
    return pl.pallas_call(
        _mod_kernel,
        grid=(depth, nj),
        in_specs=[pl.BlockSpec((b, d), lambda l, j: (0, 0)),
                  pl.BlockSpec((1, d, d), lambda l, j: (l, 0, j)),
                  pl.BlockSpec((1, 1, d), lambda l, j: (l, 0, j))],
        out_specs=pl.BlockSpec((1, b, d), lambda l, j: (l, 0, j)),
        out_shape=jax.ShapeDtypeStruct((depth, b, d6), F32),
        compiler_params=_params(("parallel", "parallel")),
        name="adaln_mod",
    )(c, w_ada, b_ada.reshape(depth, 1, d6))


def _mod_norm(x, g, sc, sh):
    y = x * lax.rsqrt(jnp.mean(x * x, axis=-1, keepdims=True) + EPS)
    return (y * g) * (1.0 + sc) + sh


def _rope(z, c, s1, s2):
    n = z.shape[1] // LANES
    if n > 1:
        c = jnp.concatenate([c] * n, axis=1)
        s1 = jnp.concatenate([s1] * n, axis=1)
        s2 = jnp.concatenate([s2] * n, axis=1)
    w = z.shape[1]
    return z * c + pltpu.roll(z, ROT_HALF, 1) * s1 + pltpu.roll(z, w - ROT_HALF, 1) * s2


def _inproj_kernel(x_ref, sc_ref, sh_ref, g_ref, w_ref, c_ref, s1_ref, s2_ref,
                   xp_ref, qe_ref, qo_ref, k_ref, v_ref, qi_ref, kw_ref, ka_ref, kb_ref):
    h = _mod_norm(x_ref[...], g_ref[...], sc_ref[0], sh_ref[0]).astype(BF16)
    c, s1, s2 = c_ref[...], s1_ref[...], s2_ref[...]

    def proj(lo, hi):
        return jnp.dot(h, w_ref[:, lo:hi], preferred_element_type=F32)

    xp_ref[...] = proj(0, 512)
    lane = lax.broadcasted_iota(I32, (h.shape[0], D_ATTN), 1)
    even = (lane % LANES) < HEAD_DIM
    q = _rope(proj(512, 1024), c, s1, s2) * (HEAD_DIM ** -0.5 * LOG2E)
    qe_ref[...] = jnp.where(even, q, 0.0).astype(BF16)
    qo_ref[...] = jnp.where(even, 0.0, q).astype(BF16)
    k_ref[...] = _rope(proj(1024, 1536), c, s1, s2).astype(BF16)
    v_ref[...] = proj(1536, 2048).astype(BF16)
    qi_ref[...] = (_rope(proj(2048, 2560), c, s1, s2) * (IDX_DIM ** -0.5)).astype(BF16)
    lane1 = lax.broadcasted_iota(I32, (h.shape[0], LANES), 1)
    is_k = lane1 < IDX_DIM
    kw = _rope(proj(2560, 2688), jnp.where(is_k, c, 1.0), jnp.where(is_k, s1, 0.0),
               jnp.where(is_k, s2, 0.0))
    kw = kw * jnp.where(is_k, 1.0, N_IDX_HEADS ** -0.5)
    kw_ref[...] = kw
    ka = jnp.where(is_k, kw, 0.0)
    ka_ref[...] = ka.astype(BF16)
    kb_ref[...] = pltpu.roll(ka, IDX_DIM, 1).astype(BF16)


def _in_projection(x2, sc, sh, g, w_pad, rc, rs1, rs2, seq, tm):
    n, d = x2.shape
    tps = seq // tm
    row = lambda w: pl.BlockSpec((tm, w), lambda i: (i, 0))
    per_b = pl.BlockSpec((1, 1, d), lambda i: (i // tps, 0, 0))
    out_w = [(512, F32)] + [(512, BF16)] * 5 + [(LANES, F32), (LANES, BF16), (LANES, BF16)]
    return pl.pallas_call(
        _inproj_kernel,
        grid=(n // tm,),
        in_specs=[row(d), per_b, per_b, pl.BlockSpec((1, d), lambda i: (0, 0)),
                  pl.BlockSpec(w_pad.shape, lambda i: (0, 0)),
                  row(LANES), row(LANES), row(LANES)],
        out_specs=[row(w) for w, _ in out_w],
        out_shape=[jax.ShapeDtypeStruct((n, w), dt) for w, dt in out_w],
        compiler_params=_params(("parallel",)),
        name="in_proj",
    )(x2, sc, sh, g, w_pad, rc, rs1, rs2)


def _select_kernel(qi_ref, kw_ref, ka_ref, kb_ref, bias_ref, keys_scr, wb_scr, qs_scr, cut_scr, *,
                   tq, sc, topk):
    iq = pl.program_id(1)
    n_total = bias_ref.shape[1]
    nck = ((iq + 1) * tq + sc - 1) // sc
    kw = kw_ref[0]
    for h in range(N_IDX_HEADS):
        wb_scr[h] = jnp.broadcast_to(kw[:, IDX_DIM + h:IDX_DIM + h + 1], (tq, LANES))
    for j in range(N_IDX_HEADS // 2):
        qs_scr[j * tq:(j + 1) * tq, :] = qi_ref[0, :, j * LANES:(j + 1) * LANES]
    nl = sc // LANES
    row = iq * tq + lax.broadcasted_iota(I32, (tq, sc), 0)
    col0 = lax.broadcasted_iota(I32, (tq, sc), 1)

    def tile(a):
        return jnp.concatenate([a] * nl, axis=1) if nl > 1 else a

    def score_body(c, carry):
        off = pl.multiple_of(c * sc, sc)
        ka = ka_ref[0, pl.ds(off, sc), :]
        kb = kb_ref[0, pl.ds(off, sc), :]
        d_even = lax.dot_general(qs_scr[...], ka, _NT, preferred_element_type=F32)
        d_odd = lax.dot_general(qs_scr[...], kb, _NT, preferred_element_type=F32)
        score = jnp.zeros((tq, sc), F32)
        for j in range(N_IDX_HEADS // 2):
            rs = slice(j * tq, (j + 1) * tq)
            score = score + jnp.maximum(d_even[rs], 0.0) * tile(wb_scr[2 * j])
            score = score + jnp.maximum(d_odd[rs], 0.0) * tile(wb_scr[2 * j + 1])
        bits = pltpu.bitcast(score, I32)
        key = bits ^ ((bits >> 31) & 0x7FFFFFFF)
        key = jnp.where(score == 0.0, 0, key)
        keys_scr[c] = jnp.where(col0 + off <= row, key, INT_MIN)
        return carry

    lax.fori_loop(0, nck, score_body, 0)

    rb = min(tq, 128)
    rb_tie = min(tq, 32)
    lane = lax.broadcasted_iota(I32, (rb_tie, LANES), 1)

    def count(hit, *row_args, rows=rb, gates=None):
        parts = []
        for b, r0 in enumerate(range(0, tq, rows)):
            blk = [a[r0:r0 + rows] for a in row_args]

            def cnt_body(c, acc, r0=r0, blk=blk):
                k = keys_scr[c, r0:r0 + rows, :]
                for j in range(nl):
                    acc = acc + hit(k[:, j * LANES:(j + 1) * LANES], c * sc + j * LANES, *blk)
                return acc

            zero = jnp.zeros((rows, LANES), F32)
            scan = functools.partial(lax.fori_loop, 0, nck, cnt_body, zero)
            parts.append(scan() if gates is None else lax.cond(gates[b], scan, lambda zero=zero: zero))
        part = jnp.concatenate(parts, axis=0) if len(parts) > 1 else parts[0]
        return jnp.sum(part, axis=1, keepdims=True)

    def bit_body(i, carry):
        th, n_ge = carry
        cand = th + (jnp.int32(1) << (31 - i))
        cnt = count(lambda k, off, cb: jnp.where(k >= cb, 1.0, 0.0), cand)
        ok = cnt >= float(topk)
        return jnp.where(ok, cand, th), jnp.where(ok, cnt, n_ge)

    th, n_ge = lax.fori_loop(0, 32, bit_body, (jnp.full((tq, LANES), INT_MIN, I32),
                                               jnp.zeros((tq, LANES), F32)))
    excess = jnp.logical_and(n_ge > float(topk), th != INT_MIN)
    cut_scr[...] = jnp.full((tq, LANES), n_total * sc, I32)

    excess_f = jnp.where(excess, 1.0, 0.0)

    @pl.when(jnp.max(excess_f) > 0.0)
    def _():
        gates = [jnp.max(excess_f[r0:r0 + rb_tie]) > 0.0 for r0 in range(0, tq, rb_tie)]
        n_gt = count(lambda k, off, tb: jnp.where(k > tb, 1.0, 0.0), th, rows=rb_tie, gates=gates)
        need = float(topk) - n_gt

        def cut_body(i, cut):
            cand = cut + (jnp.int32(n_total * sc // 2) >> i)
            before = count(lambda k, off, tb, cb: jnp.where(k == tb, jnp.where(lane + off < cb, 1.0, 0.0), 0.0),
                           th, cand, rows=rb_tie, gates=gates)
            return jnp.where(before < need, cand, cut)

        n_bits = (n_total * sc).bit_length() - 1
        cut = lax.fori_loop(0, n_bits, cut_body, jnp.zeros((tq, LANES), I32))
        cut_scr[...] = jnp.where(excess, cut, n_total * sc)

    th = tile(jnp.maximum(th, INT_MIN + 1))
    cut = tile(cut_scr[...])

    def out_body(c, carry):
        k = keys_scr[c]
        tie = jnp.where(k == th, jnp.where(col0 + c * sc <= cut, 0.0, NEG_BIG), NEG_BIG)
        bias_ref[0, c] = jnp.where(k > th, 0.0, tie).astype(BF16)
        return carry

    lax.fori_loop(0, nck, out_body, 0)

    def fill_body(c, carry):
        bias_ref[0, c] = jnp.full((tq, sc), NEG_BIG, BF16)
        return carry

    lax.fori_loop(nck, n_total, fill_body, 0)


def _select(qi, kw, ka, kb, topk, tq, sc):
    b, s, _ = qi.shape
    nq, nk = s // tq, s // sc
    kern = functools.partial(_select_kernel, tq=tq, sc=sc, topk=topk)
    return pl.pallas_call(
        kern,
        grid=(b, nq),
        in_specs=[pl.BlockSpec((1, tq, 512), lambda bi, i: (bi, i, 0)),
                  pl.BlockSpec((1, tq, LANES), lambda bi, i: (bi, i, 0)),
                  pl.BlockSpec((1, s, LANES), lambda bi, i: (bi, 0, 0)),
                  pl.BlockSpec((1, s, LANES), lambda bi, i: (bi, 0, 0))],
        out_specs=pl.BlockSpec((1, nk, tq, sc), lambda bi, i: (bi, 0, i, 0)),
        out_shape=jax.ShapeDtypeStruct((b, nk, s, sc), BF16),
        scratch_shapes=[pltpu.VMEM((nk, tq, sc), I32), pltpu.VMEM((N_IDX_HEADS, tq, LANES), F32),
                        pltpu.VMEM((N_IDX_HEADS // 2 * tq, LANES), BF16), pltpu.VMEM((tq, LANES), I32)],
        compiler_params=_params(("parallel", "arbitrary")),
        name="dsa_select",
    )(qi, kw, ka, kb)


def _attn_kernel(qe_ref, qo_ref, k_ref, v_ref, bias_ref, o_ref, m_scr, l_scr, acc_scr, *, tq, sc):
    iq = pl.program_id(1)
    c = pl.program_id(2)
    nk = pl.num_programs(2)
    last = ((iq + 1) * tq - 1) // sc
    nl = sc // LANES
    lt = lax.broadcasted_iota(I32, (tq, LANES), 1) < HEAD_DIM

    @pl.when(c == 0)
    def _():
        m_scr[...] = jnp.full(m_scr.shape, -jnp.inf, F32)
        l_scr[...] = jnp.zeros(l_scr.shape, F32)
        acc_scr[...] = jnp.zeros(acc_scr.shape, F32)

    @pl.when(c <= last)
    def _():
        bias = bias_ref[0, 0].astype(F32)
        bias2 = jnp.concatenate([bias, bias], axis=0)
        for j in range(N_HEADS // 2):
            sl = slice(j * LANES, (j + 1) * LANES)
            q2 = jnp.concatenate([qe_ref[0, :, sl], qo_ref[0, :, sl]], axis=0)
            s = lax.dot_general(q2, k_ref[0, :, sl], _NT, preferred_element_type=F32) + bias2
            m_prev = m_scr[j]
            m_new = jnp.maximum(m_prev, jnp.max(s, axis=1, keepdims=True))
            p = jnp.exp2(s - (jnp.concatenate([m_new] * nl, axis=1) if nl > 1 else m_new))
            alpha = jnp.exp2(m_prev - m_new)
            l_scr[j] = alpha * l_scr[j] + jnp.sum(p, axis=1, keepdims=True)
            m_scr[j] = m_new
            pv = jnp.dot(p.astype(BF16), v_ref[0, :, sl], preferred_element_type=F32)
            acc = acc_scr[:, sl]
            acc_scr[:, sl] = jnp.where(lt, alpha[:tq] * acc + pv[:tq], alpha[tq:] * acc + pv[tq:])

    @pl.when(c == nk - 1)
    def _():
        for j in range(N_HEADS // 2):
            sl = slice(j * LANES, (j + 1) * LANES)
            l = jnp.where(lt, l_scr[j, :tq], l_scr[j, tq:])
            o_ref[0, :, sl] = (acc_scr[:, sl] / l).astype(BF16)


def _attention(qe, qo, k, v, bias, tq, sc):
    b, s, w = qe.shape
    nq, nk = s // tq, s // sc
    last = lambda i: ((i + 1) * tq - 1) // sc
    qspec = pl.BlockSpec((1, tq, w), lambda bi, i, c: (bi, i, 0))
    kspec = pl.BlockSpec((1, sc, w), lambda bi, i, c: (bi, jnp.minimum(c, last(i)), 0))
    kern = functools.partial(_attn_kernel, tq=tq, sc=sc)
    return pl.pallas_call(
        kern,
        grid=(b, nq, nk),
        in_specs=[qspec, qspec, kspec, kspec,
                  pl.BlockSpec((1, 1, tq, sc), lambda bi, i, c: (bi, jnp.minimum(c, last(i)), i, 0))],
        out_specs=pl.BlockSpec((1, tq, w), lambda bi, i, c: (bi, i, 0)),
        out_shape=jax.ShapeDtypeStruct((b, s, w), BF16),
        scratch_shapes=[pltpu.VMEM((N_HEADS // 2, 2 * tq, LANES), F32),
                        pltpu.VMEM((N_HEADS // 2, 2 * tq, LANES), F32), pltpu.VMEM((tq, w), F32)],
        compiler_params=_params(("parallel", "parallel", "arbitrary")),
        name="dsa_attention",
    )(qe, qo, k, v, bias)


def _outproj_kernel(xp_ref, halo_ref, attn_ref, x_ref, g1_ref, wpool_ref, ps_ref, wout_ref, o_ref,
                    ext_scr, *, tm, seq):
    i = pl.program_id(0)
    t0 = (i % (seq // tm)) * tm
    ext_scr[0:POOL_HALO, :] = halo_ref[...] * jnp.where(t0 == 0, 0.0, 1.0)
    ext_scr[POOL_HALO:POOL_HALO + tm, :] = xp_ref[...]
    t = t0 + lax.broadcasted_iota(I32, (tm, POOL_GROUP), 0)
    total = jnp.dot(attn_ref[...], wout_ref[D_POOL:D_POOL + D_ATTN, :], preferred_element_type=F32)
    for g, w in enumerate(POOL_WINDOWS):
        sl = slice(g * POOL_GROUP, (g + 1) * POOL_GROUP)
        cur = ext_scr[POOL_HALO:POOL_HALO + tm, sl]
        acc = cur
        for j in range(1, w):
            acc = acc + ext_scr[POOL_HALO - j:POOL_HALO - j + tm, sl]
        cnt = jnp.minimum(t + 1, w).astype(F32)
        pooled = acc / cnt - cur
        mixed = jnp.dot(pooled.astype(BF16), wpool_ref[g], preferred_element_type=F32) * ps_ref[:, sl]
        total = total + jnp.dot(mixed.astype(BF16), wout_ref[sl, :], preferred_element_type=F32)
    o_ref[...] = x_ref[...] + g1_ref[0] * total


def _out_projection(xp, attn, x2, g1, w_pool, pscale, w_out, seq, tm):
    n, d = x2.shape
    tps = seq // tm
    hb = tm // POOL_HALO
    row = lambda w: pl.BlockSpec((tm, w), lambda i: (i, 0))
    kern = functools.partial(_outproj_kernel, tm=tm, seq=seq)
    return pl.pallas_call(
        kern,
        grid=(n // tm,),
        in_specs=[row(D_POOL),
                  pl.BlockSpec((POOL_HALO, D_POOL), lambda i: (jnp.maximum(i * hb - 1, 0), 0)),
                  row(D_ATTN), row(d),
                  pl.BlockSpec((1, 1, d), lambda i: (i // tps, 0, 0)),
                  pl.BlockSpec(w_pool.shape, lambda i: (0, 0, 0)),
                  pl.BlockSpec((1, D_POOL), lambda i: (0, 0)),
                  pl.BlockSpec(w_out.shape, lambda i: (0, 0))],
        out_specs=row(d),
        out_shape=jax.ShapeDtypeStruct((n, d), F32),
        scratch_shapes=[pltpu.VMEM((tm + POOL_HALO, D_POOL), F32)],
        compiler_params=_params(("parallel",)),
        name="pool_out_proj",
    )(xp, xp, attn, x2, g1, w_pool, pscale, w_out)


EXPERT_SUB = 512
WDT = BF16
NO_RANK = 127.0


def _top_values(w, n, want_rank=False):
    outs = []
    rank = jnp.full(w.shape, NO_RANK, F32) if want_rank else None
    for r in range(n):
        mx = jnp.max(w, axis=0, keepdims=True)
        outs.append(mx)
        hit = w == mx
        if want_rank:
            rank = jnp.where(hit, float(r), rank)
        w = jnp.where(hit, -jnp.inf, w)
    return outs, rank


def _router_kernel(x_ref, sc_ref, sh_ref, g_ref, wq_ref, sk_ref,
                   h_ref, nrow_ref, scl_ref, rb_ref, e2_ref):
    hb = _mod_norm(x_ref[...], g_ref[...], sc_ref[0], sh_ref[0]).astype(BF16)
    h_ref[...] = hb
    tt = hb.shape[0]
    for hd in range(PEER_HEADS):
        s, tops, ranks = [], [], []
        for p in range(2):
            r0 = (hd * 2 + p) * N_KEYS
            qt = lax.dot_general(wq_ref[r0:r0 + N_KEYS, :], hb, _NT, preferred_element_type=F32)
            st = jnp.dot(sk_ref[hd, p], qt.astype(BF16), preferred_element_type=F32)
            s.append(st)
            top, rank = _top_values(st, PEER_TOPK, want_rank=True)
            tops.append(top)
            ranks.append(rank)
        row = lax.broadcasted_iota(I32, (PEER_TOPK, tt), 0)
        b_mat = jnp.zeros((PEER_TOPK, tt), F32)
        for r, brow in enumerate(tops[1]):
            b_mat = jnp.where(row == r, brow, b_mat)
        a_mat = jnp.zeros((PEER_TOPK, tt), F32)
        for r, arow in enumerate(tops[0]):
            a_mat = jnp.where(row == r, arow, a_mat)
        cands = [a + b_mat for a in tops[0]]
        neg = -jnp.inf
        row8 = lax.broadcasted_iota(I32, (8, tt), 0)
        stair =[cands[0], cands[1][:8],
                 jnp.where(row8 < 5, cands[2][:8], neg), jnp.where(row8 < 4, cands[3][:8], neg),
                 jnp.where(row >= 4, a_mat + tops[1][0], neg),
                 jnp.where(jnp.logical_and(row >= 4, row < 8), a_mat + tops[1][1], neg),
                 jnp.where(row == 4, a_mat + tops[1][2], neg)]
        best, _ = _top_values(jnp.concatenate(stair, axis=0), PEER_TOPK)
        theta = best[PEER_TOPK - 1]
        z = jnp.zeros_like(theta)
        for cval in best:
            z = z + jnp.exp(cval - best[0])
        nrow = jnp.zeros((N_KEYS, tt), F32)
        for r, cnd in enumerate(cands):
            n_r = jnp.sum(jnp.where(cnd >= theta, 1.0, 0.0), axis=0, keepdims=True)
            nrow = jnp.where(ranks[0] == float(r), n_r, nrow)
        nrow_ref[hd] = nrow
        scl_ref[hd] = jnp.exp(s[0] - tops[0][0]) / z
        rb_ref[hd] = ranks[1].astype(WDT)
        e2_ref[hd] = jnp.exp(s[1] - tops[1][0]).astype(WDT)


def _router(x2, sc, sh, g, wq_t, sk, seq, tt):
    n, d = x2.shape
    tps = seq // tt
    per_b = pl.BlockSpec((1, 1, d), lambda i: (i // tps, 0, 0))
    big = pl.BlockSpec((PEER_HEADS, N_KEYS, tt), lambda i: (0, 0, i))
    big_shape = lambda dt: jax.ShapeDtypeStruct((PEER_HEADS, N_KEYS, n), dt)
    return pl.pallas_call(
        _router_kernel,
        grid=(n // tt,),
        in_specs=[pl.BlockSpec((tt, d), lambda i: (i, 0)), per_b, per_b,
                  pl.BlockSpec((1, d), lambda i: (0, 0)),
                  pl.BlockSpec(wq_t.shape, lambda i: (0, 0)),
                  pl.BlockSpec(sk.shape, lambda i: (0, 0, 0, 0))],
        out_specs=[pl.BlockSpec((tt, d), lambda i: (i, 0)), big, big, big, big],
        out_shape=[jax.ShapeDtypeStruct((n, d), BF16), big_shape(F32), big_shape(F32),
                   big_shape(WDT), big_shape(WDT)],
        compiler_params=_params(("parallel",)),
        name="peer_router",
    )(x2, sc, sh, g, wq_t, sk)


def _expert_kernel(h_ref, u_ref, vt_ref, nrow_ref, scl_ref, rb_ref, e2_ref, x_ref, g2_ref, gain_ref,
                   o_ref, acc_scr, ct_scr, *, et, final):
    j = pl.program_id(1)
    nj = pl.num_programs(1)
    tt = h_ref.shape[0]

    @pl.when(j == 0)
    def _():
        acc_scr[...] = jnp.zeros(acc_scr.shape, F32)

    def rows(ref, hd, i1):
        r = jnp.broadcast_to(ref[hd, pl.ds(i1, 1), :], (16, tt)).astype(WDT)
        return jnp.concatenate([r] * (N_KEYS // 16), axis=0)

    for s0 in range(0, et, EXPERT_SUB):
        at = lax.dot_general(u_ref[s0:s0 + EXPERT_SUB, :], h_ref[...], _NT,
                             preferred_element_type=F32)
        for k in range(EXPERT_SUB // N_KEYS):
            i1 = (j * et + s0) // N_KEYS + k
            a = at[k * N_KEYS:(k + 1) * N_KEYS, :]
            w = jnp.zeros((N_KEYS, tt), WDT)
            for hd in range(PEER_HEADS):
                sel = rb_ref[hd] < rows(nrow_ref, hd, i1)
                w = w + jnp.where(sel, e2_ref[hd], jnp.zeros((), WDT)) * rows(scl_ref, hd, i1)
            gelu = 0.5 * a * (1.0 + lax.erf(a * (2.0 ** -0.5)))
            ct_scr[s0 + k * N_KEYS:s0 + (k + 1) * N_KEYS, :] = (w * gelu.astype(WDT)).astype(BF16)
    acc_scr[...] += jnp.dot(vt_ref[...], ct_scr[...], preferred_element_type=F32)

    @pl.when(j == nj - 1)
    def _():
        y = x_ref[...] + g2_ref[0] * acc_scr[...].T
        if final:
            y = y * lax.rsqrt(jnp.mean(y * y, axis=-1, keepdims=True) + EPS) * gain_ref[...]
        o_ref[...] = y


def _experts(hb, u, vt, nrow, scl, rb, e2, x2, g2, gain, seq, tt, et, final):
    n, d = x2.shape
    ne = u.shape[0]
    tps = seq // tt
    big = pl.BlockSpec((PEER_HEADS, N_KEYS, tt), lambda i, j: (0, 0, i))
    kern = functools.partial(_expert_kernel, et=et, final=final)
    return pl.pallas_call(
        kern,
        grid=(n // tt, ne // et),
        in_specs=[pl.BlockSpec((tt, d), lambda i, j: (i, 0)),
                  pl.BlockSpec((et, d), lambda i, j: (j, 0)),
                  pl.BlockSpec((d, et), lambda i, j: (0, j)),
                  big, big, big, big,
                  pl.BlockSpec((tt, d), lambda i, j: (i, 0)),
                  pl.BlockSpec((1, 1, d), lambda i, j: (i // tps, 0, 0)),
                  pl.BlockSpec((1, d), lambda i, j: (0, 0))],
        out_specs=pl.BlockSpec((tt, d), lambda i, j: (i, 0)),
        out_shape=jax.ShapeDtypeStruct((n, d), F32),
        scratch_shapes=[pltpu.VMEM((d, tt), F32), pltpu.VMEM((et, tt), BF16)],
        compiler_params=_params(("parallel", "arbitrary")),
        name="peer_experts",
    )(hb, u, vt, nrow, scl, rb, e2, x2, g2, gain)


def _rope_tables(positions):
    inv = ROPE_THETA ** (-jnp.arange(0, 2 * ROT_HALF, 2, dtype=F32) / (2 * ROT_HALF))
    ang = positions.astype(F32).reshape(-1, 1) * inv
    cos, sin = jnp.cos(ang), jnp.sin(ang)
    d = jnp.arange(LANES) % HEAD_DIM
    cosl, sinl = cos[:, d % ROT_HALF], sin[:, d % ROT_HALF]
    c = jnp.where(d < 2 * ROT_HALF, cosl, 1.0)
    s1 = jnp.where((d >= ROT_HALF) & (d < 2 * ROT_HALF), sinl, 0.0)
    s2 = jnp.where(d < ROT_HALF, -sinl, 0.0)
    return c, s1, s2


def _tile(dim, want):
    return min(dim, want)


def kernel(x, c, positions, w_ada, b_ada, norm_mix, w_in, w_pool, pool_scale, w_out, norm_ffn,
           w_query, sub_keys, expert_u, expert_v, final_norm):
    b, s, d = x.shape
    depth = w_ada.shape[0]
    n = b * s
    topk = min(TOPK_MAX, s // 4)
    tm = _tile(s, 512)
    tq = _tile(s, 512)
    sc = _tile(s, 1024)
    tt = _tile(s, 512)
    et = 4 * EXPERT_SUB
    assert depth >= 1 and all(s % t == 0 for t in (tm, tq, sc, tt)) and tm % POOL_HALO == 0
    assert s & (s - 1) == 0
    assert expert_u.shape[1] % et == 0 and expert_u.shape[1] == N_KEYS * N_KEYS

    rc, rs1, rs2 = _rope_tables(positions)
    mod = _modulation(c, w_ada, b_ada)
    d_in = w_in.shape[2]
    pad = (-d_in) % LANES
    x2 = x.reshape(n, d)
    for l in range(depth):
        sh1, sc1, g1, sh2, sc2, g2 = [mod[l, :, i * d:(i + 1) * d].reshape(b, 1, d) for i in range(6)]
        w_pad = jnp.pad(w_in[l], ((0, 0), (0, pad))).astype(BF16)
        xp, qe, qo, k, v, qi, kw, ka, kb = _in_projection(
            x2, sc1, sh1, norm_mix[l].reshape(1, d), w_pad, rc, rs1, rs2, s, tm)
        r3 = lambda a: a.reshape(b, s, a.shape[-1])
        bias = _select(r3(qi), r3(kw), r3(ka), r3(kb), topk, tq, sc)
        attn = _attention(r3(qe), r3(qo), r3(k), r3(v), bias, tq, sc).reshape(n, D_ATTN)
        x2 = _out_projection(xp, attn, x2, g1, w_pool[l].astype(BF16), pool_scale[l].reshape(1, D_POOL),
                             w_out[l].astype(BF16), s, tm)
        hb, nrow, scl, rb, e2 = _router(x2, sc2, sh2, norm_ffn[l].reshape(1, d),
                                        w_query[l].T.astype(BF16), sub_keys[l].astype(BF16), s, tt)
        x2 = _experts(hb, expert_u[l].astype(BF16), expert_v[l].T.astype(BF16), nrow, scl, rb, e2,
                      x2, g2, final_norm.reshape(1, d), s, tt, et, final=(l == depth - 1))
    return x2.reshape(b, s, d)
```

```python
import functools

import jax
import jax.numpy as jnp
from jax import lax
from jax.experimental import pallas as pl
from jax.experimental.pallas import tpu as pltpu

F32 = jnp.float32
BF16 = jnp.bfloat16
I32 = jnp.int32

EPS = 1e-6
ROPE_THETA = 500000.0
N_HEADS = 8
HEAD_DIM = 64
ROT_HALF = 8
D_POOL = 512
D_ATTN = 512
POOL_WINDOWS = (2, 4, 8, 16)
POOL_GROUP = 128
POOL_HALO = 16
N_IDX_HEADS = 8
IDX_DIM = 64
TOPK_MAX = 256
N_KEYS = 128
PEER_HEADS = 8
PEER_TOPK = 16
LANES = 128
INT_MIN = -2147483648
LOG2E = 1.4426950408889634
NEG_BIG = -1e30
VMEM_LIMIT = 56 * 1024 * 1024

_NT = (((1,), (1,)), ((), ()))


def _params(sem):
    return pltpu.CompilerParams(dimension_semantics=sem, vmem_limit_bytes=VMEM_LIMIT)


def _mod_kernel(c_ref, w_ref, b_ref, o_ref):
    c = c_ref[...]
    s = c / (1.0 + jnp.exp(-c))
    o_ref[0] = jnp.dot(s, w_ref[0], preferred_element_type=F32,
                       precision=lax.Precision.HIGHEST) + b_ref[0]


def _modulation(c, w_ada, b_ada):
    depth, d, d6 = w_ada.shape
    b = c.shape[0]
    nj = d6 // d
    return pl.pallas_call(
        _mod_kernel,
        grid=(depth, nj),
        in_specs=[pl.BlockSpec((b, d), lambda l, j: (0, 0)),
                  pl.BlockSpec((1, d, d), lambda l, j: (l, 0, j)),
                  pl.BlockSpec((1, 1, d), lambda l, j: (l, 0, j))],
        out_specs=pl.BlockSpec((1, b, d), lambda l, j: (l, 0, j)),
        out_shape=jax.ShapeDtypeStruct((depth, b, d6), F32),
        compiler_params=_params(("parallel", "parallel")),
        name="adaln_mod",
    )(c, w_ada, b_ada.reshape(depth, 1, d6))


def _mod_norm(x, g, sc, sh):
    y = x * lax.rsqrt(jnp.mean(x * x, axis=-1, keepdims=True) + EPS)
    return (y * g) * (1.0 + sc) + sh


def _rope(z, c, s1, s2):
    n = z.shape[1] // LANES
    if n > 1:
        c = jnp.concatenate([c] * n, axis=1)
        s1 = jnp.concatenate([s1] * n, axis=1)
        s2 = jnp.concatenate([s2] * n, axis=1)
    w = z.shape[1]
    return z * c + pltpu.roll(z, ROT_HALF, 1) * s1 + pltpu.roll(z, w - ROT_HALF, 1) * s2


def _inproj_kernel(x_ref, sc_ref, sh_ref, g_ref, w_ref, c_ref, s1_ref, s2_ref,
                   xp_ref, qe_ref, qo_ref, k_ref, v_ref, qi_ref, kw_ref, ka_ref, kb_ref):
    h = _mod_norm(x_ref[...], g_ref[...], sc_ref[0], sh_ref[0]).astype(BF16)
    c, s1, s2 = c_ref[...], s1_ref[...], s2_ref[...]

    def proj(lo, hi):
        return jnp.dot(h, w_ref[:, lo:hi], preferred_element_type=F32)

    xp_ref[...] = proj(0, 512)
    lane = lax.broadcasted_iota(I32, (h.shape[0], D_ATTN), 1)
    even = (lane % LANES) < HEAD_DIM
    q = _rope(proj(512, 1024), c, s1, s2) * (HEAD_DIM ** -0.5 * LOG2E)
    qe_ref[...] = jnp.where(even, q, 0.0).astype(BF16)
    qo_ref[...] = jnp.where(even, 0.0, q).astype(BF16)
    k_ref[...] = _rope(proj(1024, 1536), c, s1, s2).astype(BF16)
    v_ref[...] = proj(1536, 2048).astype(BF16)
    qi_ref[...] = (_rope(proj(2048, 2560), c, s1, s2) * (IDX_DIM ** -0.5)).astype(BF16)
    lane1 = lax.broadcasted_iota(I32, (h.shape[0], LANES), 1)
    is_k = lane1 < IDX_DIM
    kw = _rope(proj(2560, 2688), jnp.where(is_k, c, 1.0), jnp.where(is_k, s1, 0.0),
               jnp.where(is_k, s2, 0.0))
    kw = kw * jnp.where(is_k, 1.0, N_IDX_HEADS ** -0.5)
    kw_ref[...] = kw
    ka = jnp.where(is_k, kw, 0.0)
    ka_ref[...] = ka.astype(BF16)
    kb_ref[...] = pltpu.roll(ka, IDX_DIM, 1).astype(BF16)


def _in_projection(x2, sc, sh, g, w_pad, rc, rs1, rs2, seq, tm):
    n, d = x2.shape
    tps = seq // tm
    row = lambda w: pl.BlockSpec((tm, w), lambda i: (i, 0))
    per_b = pl.BlockSpec((1, 1, d), lambda i: (i // tps, 0, 0))
    out_w = [(512, F32)] + [(512, BF16)] * 5 + [(LANES, F32), (LANES, BF16), (LANES, BF16)]
    return pl.pallas_call(
        _inproj_kernel,
        grid=(n // tm,),
        in_specs=[row(d), per_b, per_b, pl.BlockSpec((1, d), lambda i: (0, 0)),
                  pl.BlockSpec(w_pad.shape, lambda i: (0, 0)),
                  row(LANES), row(LANES), row(LANES)],
        out_specs=[row(w) for w, _ in out_w],
        out_shape=[jax.ShapeDtypeStruct((n, w), dt) for w, dt in out_w],
        compiler_params=_params(("parallel",)),
        name="in_proj",
    )(x2, sc, sh, g, w_pad, rc, rs1, rs2)


def _select_kernel(qi_ref, kw_ref, ka_ref, kb_ref, bias_ref, keys_scr, wb_scr, qs_scr, cut_scr, *,
                   tq, sc, topk):
    iq = pl.program_id(1)
    n_total = bias_ref.shape[1]
    nck = ((iq + 1) * tq + sc - 1) // sc
    kw = kw_ref[0]
    for h in range(N_IDX_HEADS):
        wb_scr[h] = jnp.broadcast_to(kw[:, IDX_DIM + h:IDX_DIM + h + 1], (tq, LANES))
    for j in range(N_IDX_HEADS // 2):
        qs_scr[j * tq:(j + 1) * tq, :] = qi_ref[0, :, j * LANES:(j + 1) * LANES]
    nl = sc // LANES
    row = iq * tq + lax.broadcasted_iota(I32, (tq, sc), 0)
    col0 = lax.broadcasted_iota(I32, (tq, sc), 1)

    def tile(a):
        return jnp.concatenate([a] * nl, axis=1) if nl > 1 else a

    def score_body(c, carry):
        off = pl.multiple_of(c * sc, sc)
        ka = ka_ref[0, pl.ds(off, sc), :]
        kb = kb_ref[0, pl.ds(off, sc), :]
        d_even = lax.dot_general(qs_scr[...], ka, _NT, preferred_element_type=F32)
        d_odd = lax.dot_general(qs_scr[...], kb, _NT, preferred_element_type=F32)
        score = jnp.zeros((tq, sc), F32)
        for j in range(N_IDX_HEADS // 2):
            rs = slice(j * tq, (j + 1) * tq)
            score = score + jnp.maximum(d_even[rs], 0.0) * tile(wb_scr[2 * j])
            score = score + jnp.maximum(d_odd[rs], 0.0) * tile(wb_scr[2 * j + 1])
        bits = pltpu.bitcast(score, I32)
        key = bits ^ ((bits >> 31) & 0x7FFFFFFF)
        key = jnp.where(score == 0.0, 0, key)
        keys_scr[c] = jnp.where(col0 + off <= row, key, INT_MIN)
        return carry

    lax.fori_loop(0, nck, score_body, 0)

    rb = min(tq, 128)
    rb_tie = min(tq, 32)
    lane = lax.broadcasted_iota(I32, (rb_tie, LANES), 1)

    def count(hit, *row_args, rows=rb, gates=None):
        parts = []
        for b, r0 in enumerate(range(0, tq, rows)):
            blk = [a[r0:r0 + rows] for a in row_args]

            def cnt_body(c, acc, r0=r0, blk=blk):
                k = keys_scr[c, r0:r0 + rows, :]
                for j in range(nl):
                    acc = acc + hit(k[:, j * LANES:(j + 1) * LANES], c * sc + j * LANES, *blk)
                return acc

            zero = jnp.zeros((rows, LANES), F32)
            scan = functools.partial(lax.fori_loop, 0, nck, cnt_body, zero)
            parts.append(scan() if gates is None else lax.cond(gates[b], scan, lambda zero=zero: zero))
        part = jnp.concatenate(parts, axis=0) if len(parts) > 1 else parts[0]
        return jnp.sum(part, axis=1, keepdims=True)

    def bit_body(i, carry):
        th, n_ge = carry
        cand = th + (jnp.int32(1) << (31 - i))
        cnt = count(lambda k, off, cb: jnp.where(k >= cb, 1.0, 0.0), cand)
        ok = cnt >= float(topk)
        return jnp.where(ok, cand, th), jnp.where(ok, cnt, n_ge)

    th, n_ge = lax.fori_loop(0, 32, bit_body, (jnp.full((tq, LANES), INT_MIN, I32),
                                               jnp.zeros((tq, LANES), F32)))
    excess = jnp.logical_and(n_ge > float(topk), th != INT_MIN)
    cut_scr[...] = jnp.full((tq, LANES), n_total * sc, I32)

    excess_f = jnp.where(excess, 1.0, 0.0)

    @pl.when(jnp.max(excess_f) > 0.0)
    def _():
        gates = [jnp.max(excess_f[r0:r0 + rb_tie]) > 0.0 for r0 in range(0, tq, rb_tie)]
        n_gt = count(lambda k, off, tb: jnp.where(k > tb, 1.0, 0.0), th, rows=rb_tie, gates=gates)
        need = float(topk) - n_gt

        def cut_body(i, cut):
            cand = cut + (jnp.int32(n_total * sc // 2) >> i)
            before = count(lambda k, off, tb, cb: jnp.where(k == tb, jnp.where(lane + off < cb, 1.0, 0.0), 0.0),
                           th, cand, rows=rb_tie, gates=gates)
            return jnp.where(before < need, cand, cut)

        n_bits = (n_total * sc).bit_length() - 1
        cut = lax.fori_loop(0, n_bits, cut_body, jnp.zeros((tq, LANES), I32))
        cut_scr[...] = jnp.where(excess, cut, n_total * sc)

    th = tile(jnp.maximum(th, INT_MIN + 1))
    cut = tile(cut_scr[...])

    def out_body(c, carry):
        k = keys_scr[c]
        tie = jnp.where(k == th, jnp.where(col0 + c * sc <= cut, 0.0, NEG_BIG), NEG_BIG)
        bias_ref[0, c] = jnp.where(k > th, 0.0, tie).astype(BF16)
        return carry

    lax.fori_loop(0, nck, out_body, 0)

    def fill_body(c, carry):
        bias_ref[0, c] = jnp.full((tq, sc), NEG_BIG, BF16)
        return carry

    lax.fori_loop(nck, n_total, fill_body, 0)


def _select(qi, kw, ka, kb, topk, tq, sc):
    b, s, _ = qi.shape
    nq, nk = s // tq, s // sc
    kern = functools.partial(_select_kernel, tq=tq, sc=sc, topk=topk)
    return pl.pallas_call(
        kern,
        grid=(b, nq),
        in_specs=[pl.BlockSpec((1, tq, 512), lambda bi, i: (bi, i, 0)),
                  pl.BlockSpec((1, tq, LANES), lambda bi, i: (bi, i, 0)),
                  pl.BlockSpec((1, s, LANES), lambda bi, i: (bi, 0, 0)),
                  pl.BlockSpec((1, s, LANES), lambda bi, i: (bi, 0, 0))],
        out_specs=pl.BlockSpec((1, nk, tq, sc), lambda bi, i: (bi, 0, i, 0)),
        out_shape=jax.ShapeDtypeStruct((b, nk, s, sc), BF16),
        scratch_shapes=[pltpu.VMEM((nk, tq, sc), I32), pltpu.VMEM((N_IDX_HEADS, tq, LANES), F32),
                        pltpu.VMEM((N_IDX_HEADS // 2 * tq, LANES), BF16), pltpu.VMEM((tq, LANES), I32)],
        compiler_params=_params(("parallel", "arbitrary")),
        name="dsa_select",
    )(qi, kw, ka, kb)


def _attn_kernel(qe_ref, qo_ref, k_ref, v_ref, bias_ref, o_ref, m_scr, l_scr, acc_scr, *, tq, sc):
    iq = pl.program_id(1)
    c = pl.program_id(2)
    nk = pl.num_programs(2)
    last = ((iq + 1) * tq - 1) // sc
    nl = sc // LANES
    lt = lax.broadcasted_iota(I32, (tq, LANES), 1) < HEAD_DIM

    @pl.when(c == 0)
    def _():
        m_scr[...] = jnp.full(m_scr.shape, -jnp.inf, F32)
        l_scr[...] = jnp.zeros(l_scr.shape, F32)
        acc_scr[...] = jnp.zeros(acc_scr.shape, F32)

    @pl.when(c <= last)
    def _():
        bias = bias_ref[0, 0].astype(F32)
        bias2 = jnp.concatenate([bias, bias], axis=0)
        for j in range(N_HEADS // 2):
            sl = slice(j * LANES, (j + 1) * LANES)
            q2 = jnp.concatenate([qe_ref[0, :, sl], qo_ref[0, :, sl]], axis=0)
            s = lax.dot_general(q2, k_ref[0, :, sl], _NT, preferred_element_type=F32) + bias2
            m_prev = m_scr[j]
            m_new = jnp.maximum(m_prev, jnp.max(s, axis=1, keepdims=True))
            p = jnp.exp2(s - (jnp.concatenate([m_new] * nl, axis=1) if nl > 1 else m_new))
            alpha = jnp.exp2(m_prev - m_new)
            l_scr[j] = alpha * l_scr[j] + jnp.sum(p, axis=1, keepdims=True)
            m_scr[j] = m_new
            pv = jnp.dot(p.astype(BF16), v_ref[0, :, sl], preferred_element_type=F32)
            acc = acc_scr[:, sl]
            acc_scr[:, sl] = jnp.where(lt, alpha[:tq] * acc + pv[:tq], alpha[tq:] * acc + pv[tq:])

    @pl.when(c == nk - 1)
    def _():
        for j in range(N_HEADS // 2):
            sl = slice(j * LANES, (j + 1) * LANES)
            l = jnp.where(lt, l_scr[j, :tq], l_scr[j, tq:])
            o_ref[0, :, sl] = (acc_scr[:, sl] / l).astype(BF16)


def _attention(qe, qo, k, v, bias, tq, sc):
    b, s, w = qe.shape
    nq, nk = s // tq, s // sc
    last = lambda i: ((i + 1) * tq - 1) // sc
    qspec = pl.BlockSpec((1, tq, w), lambda bi, i, c: (bi, i, 0))
    kspec = pl.BlockSpec((1, sc, w), lambda bi, i, c: (bi, jnp.minimum(c, last(i)), 0))
    kern = functools.partial(_attn_kernel, tq=tq, sc=sc)
    return pl.pallas_call(
        kern,
        grid=(b, nq, nk),
        in_specs=[qspec, qspec, kspec, kspec,
                  pl.BlockSpec((1, 1, tq, sc), lambda bi, i, c: (bi, jnp.minimum(c, last(i)), i, 0))],
        out_specs=pl.BlockSpec((1, tq, w), lambda bi, i, c: (bi, i, 0)),
        out_shape=jax.ShapeDtypeStruct((b, s, w), BF16),
        scratch_shapes=[pltpu.VMEM((N_HEADS // 2, 2 * tq, LANES), F32),
                        pltpu.VMEM((N_HEADS // 2, 2 * tq, LANES), F32), pltpu.VMEM((tq, w), F32)],
        compiler_params=_params(("parallel", "parallel", "arbitrary")),
        name="dsa_attention",
    )(qe, qo, k, v, bias)


def _outproj_kernel(xp_ref, halo_ref, attn_ref, x_ref, g1_ref, wpool_ref, ps_ref, wout_ref, o_ref,
                    ext_scr, *, tm, seq):
    i = pl.program_id(0)
    t0 = (i % (seq // tm)) * tm
    ext_scr[0:POOL_HALO, :] = halo_ref[...] * jnp.where(t0 == 0, 0.0, 1.0)
    ext_scr[POOL_HALO:POOL_HALO + tm, :] = xp_ref[...]
    t = t0 + lax.broadcasted_iota(I32, (tm, POOL_GROUP), 0)
    total = jnp.dot(attn_ref[...], wout_ref[D_POOL:D_POOL + D_ATTN, :], preferred_element_type=F32)
    for g, w in enumerate(POOL_WINDOWS):
        sl = slice(g * POOL_GROUP, (g + 1) * POOL_GROUP)
        cur = ext_scr[POOL_HALO:POOL_HALO + tm, sl]
        acc = cur
        for j in range(1, w):
            acc = acc + ext_scr[POOL_HALO - j:POOL_HALO - j + tm, sl]
        cnt = jnp.minimum(t + 1, w).astype(F32)
        pooled = acc / cnt - cur
        mixed = jnp.dot(pooled.astype(BF16), wpool_ref[g], preferred_element_type=F32) * ps_ref[:, sl]
        total = total + jnp.dot(mixed.astype(BF16), wout_ref[sl, :], preferred_element_type=F32)
    o_ref[...] = x_ref[...] + g1_ref[0] * total


def _out_projection(xp, attn, x2, g1, w_pool, pscale, w_out, seq, tm):
    n, d = x2.shape
    tps = seq // tm
    hb = tm // POOL_HALO
    row = lambda w: pl.BlockSpec((tm, w), lambda i: (i, 0))
    kern = functools.partial(_outproj_kernel, tm=tm, seq=seq)
    return pl.pallas_call(
        kern,
        grid=(n // tm,),
        in_specs=[row(D_POOL),
                  pl.BlockSpec((POOL_HALO, D_POOL), lambda i: (jnp.maximum(i * hb - 1, 0), 0)),
                  row(D_ATTN), row(d),
                  pl.BlockSpec((1, 1, d), lambda i: (i // tps, 0, 0)),
                  pl.BlockSpec(w_pool.shape, lambda i: (0, 0, 0)),
                  pl.BlockSpec((1, D_POOL), lambda i: (0, 0)),
                  pl.BlockSpec(w_out.shape, lambda i: (0, 0))],
        out_specs=row(d),
        out_shape=jax.ShapeDtypeStruct((n, d), F32),
        scratch_shapes=[pltpu.VMEM((tm + POOL_HALO, D_POOL), F32)],
        compiler_params=_params(("parallel",)),
        name="pool_out_proj",
    )(xp, xp, attn, x2, g1, w_pool, pscale, w_out)


EXPERT_SUB = 1024
WDT = BF16
NO_RANK = 127.0


def _top_values(w, n, want_rank=False):
    outs = []
    rank = jnp.full(w.shape, NO_RANK, F32) if want_rank else None
    for r in range(n):
        mx = jnp.max(w, axis=0, keepdims=True)
        outs.append(mx)
        hit = w == mx
        if want_rank:
            rank = jnp.where(hit, float(r), rank)
        w = jnp.where(hit, -jnp.inf, w)
    return outs, rank


def _router_kernel(x_ref, sc_ref, sh_ref, g_ref, wq_ref, sk_ref,
                   h_ref, nrow_ref, scl_ref, rb_ref, e2_ref):
    hb = _mod_norm(x_ref[...], g_ref[...], sc_ref[0], sh_ref[0]).astype(BF16)
    h_ref[...] = hb
    tt = hb.shape[0]
    for hd in range(PEER_HEADS):
        s, tops, ranks = [], [], []
        for p in range(2):
            r0 = (hd * 2 + p) * N_KEYS
            qt = lax.dot_general(wq_ref[r0:r0 + N_KEYS, :], hb, _NT, preferred_element_type=F32)
            st = jnp.dot(sk_ref[hd, p], qt.astype(BF16), preferred_element_type=F32)
            s.append(st)
            top, rank = _top_values(st, PEER_TOPK, want_rank=True)
            tops.append(top)
            ranks.append(rank)
        row = lax.broadcasted_iota(I32, (PEER_TOPK, tt), 0)
        b_mat = jnp.zeros((PEER_TOPK, tt), F32)
        for r, brow in enumerate(tops[1]):
            b_mat = jnp.where(row == r, brow, b_mat)
        a_mat = jnp.zeros((PEER_TOPK, tt), F32)
        for r, arow in enumerate(tops[0]):
            a_mat = jnp.where(row == r, arow, a_mat)
        cands = [a + b_mat for a in tops[0]]
        neg = -jnp.inf
        row8 = lax.broadcasted_iota(I32, (8, tt), 0)
        stair =[cands[0], cands[1][:8],
                 jnp.where(row8 < 5, cands[2][:8], neg), jnp.where(row8 < 4, cands[3][:8], neg),
                 jnp.where(row >= 4, a_mat + tops[1][0], neg),
                 jnp.where(jnp.logical_and(row >= 4, row < 8), a_mat + tops[1][1], neg),
                 jnp.where(row == 4, a_mat + tops[1][2], neg)]
        best, _ = _top_values(jnp.concatenate(stair, axis=0), PEER_TOPK)
        theta = best[PEER_TOPK - 1]
        z = jnp.zeros_like(theta)
        for cval in best:
            z = z + jnp.exp(cval - best[0])
        nrow = jnp.zeros((N_KEYS, tt), F32)
        for r, cnd in enumerate(cands):
            n_r = jnp.sum(jnp.where(cnd >= theta, 1.0, 0.0), axis=0, keepdims=True)
            nrow = jnp.where(ranks[0] == float(r), n_r, nrow)
        nrow_ref[hd] = nrow
        scl_ref[hd] = jnp.exp(s[0] - tops[0][0]) / z
        rb_ref[hd] = ranks[1].astype(WDT)
        e2_ref[hd] = jnp.exp(s[1] - tops[1][0]).astype(WDT)


def _router(x2, sc, sh, g, wq_t, sk, seq, tt):
    n, d = x2.shape
    tps = seq // tt
    per_b = pl.BlockSpec((1, 1, d), lambda i: (i // tps, 0, 0))
    big = pl.BlockSpec((PEER_HEADS, N_KEYS, tt), lambda i: (0, 0, i))
    big_shape = lambda dt: jax.ShapeDtypeStruct((PEER_HEADS, N_KEYS, n), dt)
    return pl.pallas_call(
        _router_kernel,
        grid=(n // tt,),
        in_specs=[pl.BlockSpec((tt, d), lambda i: (i, 0)), per_b, per_b,
                  pl.BlockSpec((1, d), lambda i: (0, 0)),
                  pl.BlockSpec(wq_t.shape, lambda i: (0, 0)),
                  pl.BlockSpec(sk.shape, lambda i: (0, 0, 0, 0))],
        out_specs=[pl.BlockSpec((tt, d), lambda i: (i, 0)), big, big, big, big],
        out_shape=[jax.ShapeDtypeStruct((n, d), BF16), big_shape(F32), big_shape(F32),
                   big_shape(WDT), big_shape(WDT)],
        compiler_params=_params(("parallel",)),
        name="peer_router",
    )(x2, sc, sh, g, wq_t, sk)


def _expert_kernel(h_ref, u_ref, vt_ref, nrow_ref, scl_ref, rb_ref, e2_ref, x_ref, g2_ref, gain_ref,
                   o_ref, acc_scr, ct_scr, *, et, final):
    j = pl.program_id(1)
    nj = pl.num_programs(1)
    tt = h_ref.shape[0]

    @pl.when(j == 0)
    def _():
        acc_scr[...] = jnp.zeros(acc_scr.shape, F32)

    def rows(ref, hd, i1):
        r = jnp.broadcast_to(ref[hd, pl.ds(i1, 1), :], (16, tt)).astype(WDT)
        return jnp.concatenate([r] * (N_KEYS // 16), axis=0)

    for s0 in range(0, et, EXPERT_SUB):
        at = lax.dot_general(u_ref[s0:s0 + EXPERT_SUB, :], h_ref[...], _NT,
                             preferred_element_type=F32)
        for k in range(EXPERT_SUB // N_KEYS):
            i1 = (j * et + s0) // N_KEYS + k
            a = at[k * N_KEYS:(k + 1) * N_KEYS, :]
            w = jnp.zeros((N_KEYS, tt), WDT)
            for hd in range(PEER_HEADS):
                sel = rb_ref[hd] < rows(nrow_ref, hd, i1)
                w = w + jnp.where(sel, e2_ref[hd], jnp.zeros((), WDT)) * rows(scl_ref, hd, i1)
            gelu = 0.5 * a * (1.0 + lax.erf(a * (2.0 ** -0.5)))
            ct_scr[s0 + k * N_KEYS:s0 + (k + 1) * N_KEYS, :] = (w * gelu.astype(WDT)).astype(BF16)
    acc_scr[...] += jnp.dot(vt_ref[...], ct_scr[...], preferred_element_type=F32)

    @pl.when(j == nj - 1)
    def _():
        y = x_ref[...] + g2_ref[0] * acc_scr[...].T
        if final:
            y = y * lax.rsqrt(jnp.mean(y * y, axis=-1, keepdims=True) + EPS) * gain_ref[...]
        o_ref[...] = y


def _experts(hb, u, vt, nrow, scl, rb, e2, x2, g2, gain, seq, tt, et, final):
    n, d = x2.shape
    ne = u.shape[0]
    tps = seq // tt
    big = pl.BlockSpec((PEER_HEADS, N_KEYS, tt), lambda i, j: (0, 0, i))
    kern = functools.partial(_expert_kernel, et=et, final=final)
    return pl.pallas_call(
        kern,
        grid=(n // tt, ne // et),
        in_specs=[pl.BlockSpec((tt, d), lambda i, j: (i, 0)),
                  pl.BlockSpec((et, d), lambda i, j: (j, 0)),
                  pl.BlockSpec((d, et), lambda i, j: (0, j)),
                  big, big, big, big,
                  pl.BlockSpec((tt, d), lambda i, j: (i, 0)),
                  pl.BlockSpec((1, 1, d), lambda i, j: (i // tps, 0, 0)),
                  pl.BlockSpec((1, d), lambda i, j: (0, 0))],
        out_specs=pl.BlockSpec((tt, d), lambda i, j: (i, 0)),
        out_shape=jax.ShapeDtypeStruct((n, d), F32),
        scratch_shapes=[pltpu.VMEM((d, tt), F32), pltpu.VMEM((et, tt), BF16)],
        compiler_params=_params(("parallel", "arbitrary")),
        name="peer_experts",
    )(hb, u, vt, nrow, scl, rb, e2, x2, g2, gain)


def _rope_tables(positions):
    inv = ROPE_THETA ** (-jnp.arange(0, 2 * ROT_HALF, 2, dtype=F32) / (2 * ROT_HALF))
    ang = positions.astype(F32).reshape(-1, 1) * inv
    cos, sin = jnp.cos(ang), jnp.sin(ang)
    d = jnp.arange(LANES) % HEAD_DIM
    cosl, sinl = cos[:, d % ROT_HALF], sin[:, d % ROT_HALF]
    c = jnp.where(d < 2 * ROT_HALF, cosl, 1.0)
    s1 = jnp.where((d >= ROT_HALF) & (d < 2 * ROT_HALF), sinl, 0.0)
    s2 = jnp.where(d < ROT_HALF, -sinl, 0.0)
    return c, s1, s2


def _tile(dim, want):
    return min(dim, want)


def kernel(x, c, positions, w_ada, b_ada, norm_mix, w_in, w_pool, pool_scale, w_out, norm_ffn,
           w_query, sub_keys, expert_u, expert_v, final_norm):
    b, s, d = x.shape
    depth = w_ada.shape[0]
    n = b * s
    topk = min(TOPK_MAX, s // 4)
    tm = _tile(s, 512)
    tq = _tile(s, 512)
    sc = _tile(s, 1024)
    tt = _tile(s, 512)
    et = 2 * EXPERT_SUB
    assert depth >= 1 and all(s % t == 0 for t in (tm, tq, sc, tt)) and tm % POOL_HALO == 0
    assert s & (s - 1) == 0
    assert expert_u.shape[1] % et == 0 and expert_u.shape[1] == N_KEYS * N_KEYS

    rc, rs1, rs2 = _rope_tables(positions)
    mod = _modulation(c, w_ada, b_ada)
    d_in = w_in.shape[2]
    pad = (-d_in) % LANES
    x2 = x.reshape(n, d)
    for l in range(depth):
        sh1, sc1, g1, sh2, sc2, g2 = [mod[l, :, i * d:(i + 1) * d].reshape(b, 1, d) for i in range(6)]
        w_pad = jnp.pad(w_in[l], ((0, 0), (0, pad))).astype(BF16)
        xp, qe, qo, k, v, qi, kw, ka, kb = _in_projection(
            x2, sc1, sh1, norm_mix[l].reshape(1, d), w_pad, rc, rs1, rs2, s, tm)
        r3 = lambda a: a.reshape(b, s, a.shape[-1])
        bias = _select(r3(qi), r3(kw), r3(ka), r3(kb), topk, tq, sc)
        attn = _attention(r3(qe), r3(qo), r3(k), r3(v), bias, _tile(s, 1024), sc).reshape(n, D_ATTN)
        x2 = _out_projection(xp, attn, x2, g1, w_pool[l].astype(BF16), pool_scale[l].reshape(1, D_POOL),
                             w_out[l].astype(BF16), s, tm)
        hb, nrow, scl, rb, e2 = _router(x2, sc2, sh2, norm_ffn[l].reshape(1, d),
                                        w_query[l].T.astype(BF16), sub_keys[l].astype(BF16), s, tt)
        x2 = _experts(hb, expert_u[l].astype(BF16), expert_v[l].T.astype(BF16), nrow, scl, rb, e2,
                      x2, g2, final_norm.reshape(1, d), s, tt, et, final=(l == depth - 1))
    return x2.reshape(b, s, d)
```
